```python
import jax, jax.numpy as jnp
from jax import lax
import numpy as np

D_MODEL = 1024
BATCH = 32
SEQ = 256
DEPTH = 2
DEC_BATCH = 4
DEC_SEQ = 2048
PAST_LEN = 256

GRID_W = 64
EPS = 1e-6
W_A = D_MODEL // 4
CONV_A_K = 31
H_B = 4
HGRN_HD = 64
W_B = H_B * HGRN_HD
CHUNK = 32
W_C = D_MODEL // 4
CONV_C_K = 3
W_D = D_MODEL // 4
POOL_WINDOWS = (2, 4, 8, 16)
POOL_GROUPS = 4
POOL_GC = W_D // POOL_GROUPS
POOL_OUT = D_MODEL // POOL_GROUPS
N_BRANCH = 4
D_FF = ((8 * D_MODEL + 3 * 256 - 1) // (3 * 256)) * 256
IN_WIDTHS = (W_A, W_A, W_B, W_B, W_B, W_B, W_B, W_C, W_C, W_C, W_D, N_BRANCH * D_MODEL)
N_IN = sum(IN_WIDTHS)

kernel_name = "hybrid_conv_hgrn2_pool_diffusion_step"


def rmsnorm(x, g):
    xf = x.astype(jnp.float32)
    y = xf * lax.rsqrt(jnp.mean(xf * xf, axis=-1, keepdims=True) + EPS)
    return y.astype(x.dtype) * g


def layernorm(x, g, b):
    xf = x.astype(jnp.float32)
    mu = jnp.mean(xf, axis=-1, keepdims=True)
    var = jnp.mean(jnp.square(xf - mu), axis=-1, keepdims=True)
    return ((xf - mu) * lax.rsqrt(var + EPS)).astype(x.dtype) * g + b


def along_grid(fn, u, rows, vertical):
    if rows is None:
        return fn(u)
    B, L, C = u.shape
    g = u.reshape(B, rows, GRID_W, C)
    if vertical:
        g = g.transpose(0, 2, 1, 3).reshape(B * GRID_W, rows, C)
        g = fn(g).reshape(B, GRID_W, rows, C).transpose(0, 2, 1, 3)
    else:
        g = fn(g.reshape(B * rows, GRID_W, C)).reshape(B, rows, GRID_W, C)
    return g.reshape(B, L, C)


def depthwise_conv(u, w):
    K, C = w.shape
    return lax.conv_general_dilated(
        u, w.astype(u.dtype)[:, None, :], window_strides=(1,),
        padding=[(K // 2, K // 2)], dimension_numbers=("NWC", "WIO", "NWC"),
        feature_group_count=C)


def multiscale_pool(u):
    N, L, C = u.shape
    uf = u.astype(jnp.float32)
    cs = jnp.concatenate([jnp.zeros((N, 1, C), jnp.float32), jnp.cumsum(uf, axis=1)], axis=1)
    t = jnp.arange(L)
    outs = []
    for gi, w in enumerate(POOL_WINDOWS):
        lo = jnp.clip(t - w // 2, 0, L)
        hi = jnp.clip(t + w - w // 2, 0, L)
        csg = cs[..., gi * POOL_GC:(gi + 1) * POOL_GC]
        cnt = (hi - lo).astype(jnp.float32)[None, :, None]
        mean = (csg[:, hi] - csg[:, lo]) / cnt
        outs.append(mean - uf[..., gi * POOL_GC:(gi + 1) * POOL_GC])
    return jnp.concatenate(outs, axis=-1).astype(u.dtype)


def log_forget(z, lb):
    zf = z.astype(jnp.float32)
    return jnp.logaddexp(jnp.log(lb), jnp.log1p(-lb) + jax.nn.log_sigmoid(zf))


def hgrn_chunk_scan(q, logf, v, s0):
    B, L, H, Dk = q.shape
    n = L // CHUNK

    def chunks(a):
        return a.astype(jnp.float32).reshape(B, n, CHUNK, H, a.shape[-1]).transpose(0, 3, 1, 2, 4)

    qc, lf, vc = chunks(q), chunks(logf), chunks(v)
    kc = -jnp.expm1(lf)
    bcum = jnp.cumsum(lf, axis=3)
    blast = bcum[..., -1:, :]
    mask = jnp.tril(jnp.ones((CHUNK, CHUNK), dtype=bool))
    diff = bcum[..., :, None, :] - bcum[..., None, :, :]
    decay = jnp.exp(jnp.where(mask[:, :, None], diff, -jnp.inf))
    scores = jnp.einsum("bhntk,bhntsk,bhnsk->bhnts", qc, decay, kc)
    o_intra = jnp.einsum("bhnts,bhnsv->bhntv", scores, vc)
    chunk_kv = jnp.einsum("bhnsk,bhnsv->bhnkv", kc * jnp.exp(blast - bcum), vc)
    chunk_decay = jnp.exp(blast[..., 0, :])

    def step(S, xs):
        dec, kv = xs
        return dec[..., None] * S + kv, S

    s_final, s_starts = lax.scan(
        step, s0.astype(jnp.float32),
        (chunk_decay.transpose(2, 0, 1, 3), chunk_kv.transpose(2, 0, 1, 3, 4)))
    o_inter = jnp.einsum("bhntk,nbhkv->bhntv", qc * jnp.exp(bcum), s_starts)
    o = (o_intra + o_inter).transpose(0, 2, 3, 1, 4).reshape(B, L, H, v.shape[-1])
    return o, s_final


def hgrn2_branch(q, f_fw, f_bw, i_in, g_out, s0, lb, norm_g, w_out):
    B, L, _ = q.shape
    heads = lambda t: t.reshape(B, L, H_B, HGRN_HD)
    flip = lambda t: jnp.flip(t, axis=1)
    qh, vh = heads(q), heads(i_in)
    lf_fw, lf_bw = heads(log_forget(f_fw, lb[0])), heads(log_forget(f_bw, lb[1]))
    o_fw, s_fw = hgrn_chunk_scan(qh, lf_fw, vh, s0[:, 0])
    o_bw, s_bw = hgrn_chunk_scan(flip(qh), flip(lf_bw), flip(vh), s0[:, 1])
    o = o_fw + flip(o_bw)
    o = o * lax.rsqrt(jnp.mean(o * o, axis=-1, keepdims=True) + EPS)
    o = o.reshape(B, L, W_B).astype(q.dtype) * norm_g * jax.nn.silu(g_out)
    return o @ w_out, jnp.stack([s_fw, s_bw], axis=1)


def token_mixers(h, s0, rows, lb, p):
    B, L, _ = h.shape
    offsets = np.cumsum(IN_WIDTHS)[:-1].tolist()
    z = h @ p["w_in"]
    a_val, a_gate, q, f_fw, f_bw, i_in, g_out, c_b, c_c, c_x, d_in, merge = jnp.split(z, offsets, axis=-1)
    u = a_val * jax.nn.sigmoid(a_gate)
    u = along_grid(lambda t: depthwise_conv(t, p["conv_a_w"]), u, rows, vertical=False) + p["conv_a_b"]
    y_a = jax.nn.silu(layernorm(u, p["ln_a_g"], p["ln_a_b"])) @ p["w_out_a"]
    y_b, s_final = hgrn2_branch(q, f_fw, f_bw, i_in, g_out, s0, lb, p["hgrn_norm_g"], p["w_out_b"])
    u = along_grid(lambda t: depthwise_conv(t, p["conv_c_w"]), c_c * c_x, rows, vertical=True)
    y_c = (c_b * u) @ p["w_out_c"]
    u = along_grid(multiscale_pool, d_in, rows, vertical=True)
    y_d = jnp.einsum("blgc,gcd->blgd", u.reshape(B, L, POOL_GROUPS, POOL_GC), p["pool_w"])
    y_d = y_d.reshape(B, L, D_MODEL) * p["pool_scale"]
    gates = jax.nn.sigmoid(merge).reshape(B, L, N_BRANCH, D_MODEL)
    merged = gates[..., 0, :] * y_a + gates[..., 1, :] * y_b + gates[..., 2, :] * y_c + gates[..., 3, :] * y_d
    return merged @ p["w_o"], s_final


def swiglu(h, w13, w2):
    gu = h @ w13
    gate, up = jnp.split(gu, 2, axis=-1)
    return (jax.nn.silu(gate) * up) @ w2


def trunk_layer(x, mod, s0, rows, lb, p):
    sh_m, sc_m, g_m, sh_f, sc_f, g_f = jnp.split(mod, 6, axis=-1)
    h = rmsnorm(x, p["norm_mix_g"]) * (1 + sc_m) + sh_m
    mix, s_final = token_mixers(h, s0, rows, lb, p)
    x = x + g_m * mix
    h = rmsnorm(x, p["norm_ffn_g"]) * (1 + sc_f) + sh_f
    x = x + g_f * swiglu(h, p["ffn_w13"], p["ffn_w2"])
    return x, s_final


def setup_inputs(seed: int = 0) -> dict:
    key = jax.random.key(seed)
    ks = jax.random.split(key, 32)

    def nrm(k, shape, scale):
        return jax.random.normal(k, shape, jnp.float32) * scale

    D = D_MODEL
    return {
        "x_prompt": nrm(ks[0], (BATCH, SEQ, D), 1.0),
        "x_sample": nrm(ks[1], (DEC_BATCH, DEC_SEQ, D), 1.0),
        "state_hgrn": nrm(ks[2], (DEC_BATCH, DEPTH, 2, H_B, HGRN_HD, HGRN_HD), 0.5),
        "c": nrm(ks[3], (DEC_BATCH, D), 1.0),
        "c_ctx": nrm(ks[4], (D,), 1.0),
        "ada_w": nrm(ks[5], (DEPTH, D, 6 * D), 0.5 * D ** -0.5),
        "ada_b": nrm(ks[6], (DEPTH, 6 * D), 0.02),
        "norm_mix_g": 1.0 + nrm(ks[7], (DEPTH, D), 0.05),
        "w_in": nrm(ks[8], (DEPTH, D, N_IN), D ** -0.5),
        "conv_a_w": nrm(ks[9], (DEPTH, CONV_A_K, W_A), CONV_A_K ** -0.5),
        "conv_a_b": nrm(ks[10], (DEPTH, W_A), 0.02),
        "ln_a_g": 1.0 + nrm(ks[11], (DEPTH, W_A), 0.05),
        "ln_a_b": nrm(ks[12], (DEPTH, W_A), 0.02),
        "w_out_a": nrm(ks[13], (DEPTH, W_A, D), W_A ** -0.5),
        "hgrn_lb_logits": nrm(ks[14], (DEPTH, 2, W_B), 0.5),
        "hgrn_norm_g": 1.0 + nrm(ks[15], (DEPTH, W_B), 0.05),
        "w_out_b": nrm(ks[16], (DEPTH, W_B, D), W_B ** -0.5),
        "conv_c_w": nrm(ks[17], (DEPTH, CONV_C_K, W_C), CONV_C_K ** -0.5),
        "w_out_c": nrm(ks[18], (DEPTH, W_C, D), W_C ** -0.5),
        "pool_w": nrm(ks[19], (DEPTH, POOL_GROUPS, POOL_GC, POOL_OUT), POOL_GC ** -0.5),
        "pool_scale": 1.0 + nrm(ks[20], (DEPTH, D), 0.1),
        "w_o": nrm(ks[21], (DEPTH, D, D), D ** -0.5),
        "norm_ffn_g": 1.0 + nrm(ks[22], (DEPTH, D), 0.05),
        "ffn_w13": nrm(ks[23], (DEPTH, D, 2 * D_FF), D ** -0.5),
        "ffn_w2": nrm(ks[24], (DEPTH, D_FF, D), D_FF ** -0.5),
        "final_norm_g": 1.0 + nrm(ks[25], (D,), 0.05),
    }


def reference(x_prompt, x_sample, state_hgrn, c, c_ctx, ada_w, ada_b, norm_mix_g, w_in,
              conv_a_w, conv_a_b, ln_a_g, ln_a_b, w_out_a, hgrn_lb_logits, hgrn_norm_g,
              w_out_b, conv_c_w, w_out_c, pool_w, pool_scale, w_o, norm_ffn_g,
              ffn_w13, ffn_w2, final_norm_g):
    rows = x_sample.shape[1] // GRID_W
    lb_all = jnp.cumsum(jax.nn.softmax(hgrn_lb_logits.astype(jnp.float32), axis=0), axis=0)
    lb_all = lb_all - lb_all[:1]

    y_p = x_prompt
    y_s = x_sample
    ctx_states = []
    for l in range(DEPTH):
        p = {
            "norm_mix_g": norm_mix_g[l], "w_in": w_in[l], "conv_a_w": conv_a_w[l],
            "conv_a_b": conv_a_b[l], "ln_a_g": ln_a_g[l], "ln_a_b": ln_a_b[l],
            "w_out_a": w_out_a[l], "hgrn_norm_g": hgrn_norm_g[l], "w_out_b": w_out_b[l],
            "conv_c_w": conv_c_w[l], "w_out_c": w_out_c[l], "pool_w": pool_w[l],
            "pool_scale": pool_scale[l], "w_o": w_o[l], "norm_ffn_g": norm_ffn_g[l],
            "ffn_w13": ffn_w13[l], "ffn_w2": ffn_w2[l],
        }
        mod_ctx = (jax.nn.silu(c_ctx) @ ada_w[l] + ada_b[l])[None, None, :]
        s0_ctx = jnp.zeros((y_p.shape[0], 2, H_B, HGRN_HD, HGRN_HD), jnp.float32)
        y_p, s_ctx = trunk_layer(y_p, mod_ctx, s0_ctx, None, lb_all[l], p)
        ctx_states.append(s_ctx.astype(x_prompt.dtype))
        mod_lat = (jax.nn.silu(c) @ ada_w[l] + ada_b[l])[:, None, :]
        y_s, _ = trunk_layer(y_s, mod_lat, state_hgrn[:, l], rows, lb_all[l], p)

    y_prompt = rmsnorm(y_p, final_norm_g)
    y_sample = rmsnorm(y_s, final_norm_g)
    new_state_hgrn = jnp.stack(ctx_states, axis=1)
    return (y_prompt, y_sample, new_state_hgrn)
```

```python
import functools

import jax
import jax.numpy as jnp
from jax import lax
from jax.experimental import pallas as pl
from jax.experimental.pallas import tpu as pltpu

F32 = jnp.float32
BF16 = jnp.bfloat16

D_MODEL = 1024
GRID_W = 64
EPS = 1e-6
W_BR = 256
CONV_A_K = 31
H_B = 4
HGRN_HD = 64
POOL_WINDOWS = (2, 4, 8, 16)
POOL_GC = 64
D_FF = 2816
N_MIX = 11 * W_BR
N_Z = 9 * W_BR
CTX_SEQ = 256
LAT_SEQ = 2048

HG_CHUNK = 16
HG_TILE = 128
VMEM_LIMIT = 56 * 1024 * 1024


def _cparams(n_axes=1):
    return pltpu.CompilerParams(dimension_semantics=("arbitrary",) * n_axes, vmem_limit_bytes=VMEM_LIMIT)


def _sigmoid(x):
    return 1.0 / (1.0 + jnp.exp(-x))


def _silu(x):
    return x * _sigmoid(x)


def _rmsnorm(x, g):
    return x * lax.rsqrt(jnp.mean(x * x, axis=-1, keepdims=True) + EPS) * g


def _dot(a, b):
    return jnp.dot(a, b, preferred_element_type=F32)


def _mod_kernel(c_ref, w_ref, b_ref, o_ref):
    s = _silu(c_ref[...]).astype(BF16)
    o_ref[0] = _dot(s, w_ref[0].astype(BF16)) + b_ref[0]


def _modulation(c_rows, ada_w, ada_b):
    depth, d, n = ada_w.shape
    nr = c_rows.shape[0]
    tn = 768
    return pl.pallas_call(
        _mod_kernel,
        grid=(depth, n // tn),
        in_specs=[
            pl.BlockSpec((nr, d), lambda l, j: (0, 0)),
            pl.BlockSpec((1, d, tn), lambda l, j: (l, 0, j)),
            pl.BlockSpec((1, 1, tn), lambda l, j: (l, 0, j)),
        ],
        out_specs=pl.BlockSpec((1, nr, tn), lambda l, j: (l, 0, j)),
        out_shape=jax.ShapeDtypeStruct((depth, nr, n), F32),
        compiler_params=_cparams(2),
        name="modulation",
    )(c_rows, ada_w, ada_b.reshape(depth, 1, n))


def _log_forget(z, lb):
    ls = jnp.minimum(z, 0.0) - jnp.log1p(jnp.exp(-jnp.abs(z)))
    a = jnp.log(lb)
    b = jnp.log1p(-lb) + ls
    m = jnp.maximum(a, b)
    return m + jnp.log(jnp.exp(a - m) + jnp.exp(b - m))


def _inproj_kernel(x_ref, mod_ref, g_ref, w_ref, lb_ref, h_ref, z_ref):
    d, w = D_MODEL, W_BR
    mod = mod_ref[0]
    sh, sc = mod[:, 0:d], mod[:, d:2 * d]
    h = (_rmsnorm(x_ref[...], g_ref[...]) * (1.0 + sc) + sh).astype(BF16)
    h_ref[...] = h

    def proj(j):
        return _dot(h, w_ref[:, j * w:(j + 1) * w])

    lb = lb_ref[...]
    z_ref[:, 0 * w:1 * w] = proj(0) * _sigmoid(proj(1))
    z_ref[:, 1 * w:2 * w] = proj(2)
    z_ref[:, 2 * w:3 * w] = _log_forget(proj(3), lb[:, 0:w])
    z_ref[:, 3 * w:4 * w] = _log_forget(proj(4), lb[:, w:2 * w])
    z_ref[:, 4 * w:5 * w] = proj(5)
    z_ref[:, 5 * w:6 * w] = _silu(proj(6))
    z_ref[:, 6 * w:7 * w] = proj(7)
    z_ref[:, 7 * w:8 * w] = proj(8) * proj(9)
    z_ref[:, 8 * w:9 * w] = proj(10)


def _inproj(x, mod, g, w_mix, lb, tokens_per_mod):
    t, d = x.shape
    tm = 512
    return pl.pallas_call(
        _inproj_kernel,
        grid=(t // tm,),
        in_specs=[
            pl.BlockSpec((tm, d), lambda i: (i, 0)),
            pl.BlockSpec((1, 1, 6 * d), lambda i: ((i * tm) // tokens_per_mod, 0, 0)),
            pl.BlockSpec((1, d), lambda i: (0, 0)),
            pl.BlockSpec((d, N_MIX), lambda i: (0, 0)),
            pl.BlockSpec((1, 2 * W_BR), lambda i: (0, 0)),
        ],
        out_specs=[
            pl.BlockSpec((tm, d), lambda i: (i, 0)),
            pl.BlockSpec((tm, N_Z), lambda i: (i, 0)),
        ],
        out_shape=[jax.ShapeDtypeStruct((t, d), BF16), jax.ShapeDtypeStruct((t, N_Z), F32)],
        compiler_params=_cparams(),
        name="inproj",
    )(x, mod, g, w_mix, lb)


def _row_index(rows):
    return lax.broadcasted_iota(jnp.int32, (rows, W_BR), 0)


def _shift_rows(x, j, pos, seg):
    if j == 0:
        return x
    y = pltpu.roll(x, (-j) % x.shape[0], axis=0)
    valid = (pos < seg - j) if j > 0 else (pos >= -j)
    return jnp.where(valid, y, 0.0)


def _conv_a_ln(x, pos, seg, wa_ref, ba_ref, lg_ref, lbeta_ref):
    half = CONV_A_K // 2
    acc = x * wa_ref[half:half + 1, :]
    for j in range(-half, half + 1):
        if j != 0:
            acc = acc + _shift_rows(x, j, pos, seg) * wa_ref[half + j:half + j + 1, :]
    u = acc + ba_ref[...]
    mu = jnp.mean(u, axis=-1, keepdims=True)
    var = jnp.mean(jnp.square(u - mu), axis=-1, keepdims=True)
    return _silu((u - mu) * lax.rsqrt(var + EPS) * lg_ref[...] + lbeta_ref[...])


def _pool_lane_half():
    lane = lax.broadcasted_iota(jnp.int32, (1, W_BR), 1)
    return jnp.left_shift(1, lane // POOL_GC)


def _mix_ctx_kernel(ua_ref, cb_ref, cc_ref, d_ref, wa_ref, ba_ref, lg_ref, lbeta_ref, wc_ref,
                    oa_ref, oc_ref, od_ref):
    seg = CTX_SEQ
    pos = _row_index(seg)
    half = _pool_lane_half()
    max_half = max(POOL_WINDOWS) // 2

    def body(s, carry):
        rows = pl.ds(pl.multiple_of(s * seg, seg), seg)
        oa_ref[rows, :] = _conv_a_ln(ua_ref[rows, :], pos, seg, wa_ref, ba_ref, lg_ref, lbeta_ref).astype(BF16)
        cc = cc_ref[rows, :]
        u = (_shift_rows(cc, -1, pos, seg) * wc_ref[0:1, :] + cc * wc_ref[1:2, :]
             + _shift_rows(cc, 1, pos, seg) * wc_ref[2:3, :])
        oc_ref[rows, :] = (cb_ref[rows, :] * u).astype(BF16)
        dd = d_ref[rows, :]
        acc = jnp.zeros_like(dd)
        cnt = jnp.zeros_like(dd)
        for j in range(-max_half, max_half):
            lane_on = (j >= -half) & (j <= half - 1)
            valid = (pos + j >= 0) & (pos + j < seg) & lane_on
            acc = acc + jnp.where(valid, _shift_rows(dd, j, pos, seg), 0.0)
            cnt = cnt + jnp.where(valid, 1.0, 0.0)
        od_ref[rows, :] = (acc / cnt - dd).astype(BF16)
        return carry

    lax.fori_loop(0, ua_ref.shape[0] // seg, body, 0)


def _mix_lat_kernel(ua_ref, cb_ref, cc_ref, d_ref, wa_ref, ba_ref, lg_ref, lbeta_ref, wc_ref,
                    oa_ref, oc_ref, od_ref):
    seg = GRID_W
    n_rows = LAT_SEQ // GRID_W
    pos = _row_index(seg)
    half = _pool_lane_half()
    max_half = max(POOL_WINDOWS) // 2

    def grid_row(ref, r, j):
        rr = jnp.clip(r + j, 0, n_rows - 1)
        ok = jnp.logical_and(r + j >= 0, r + j < n_rows).astype(F32)
        return ref[pl.ds(pl.multiple_of(rr * seg, seg), seg), :], ok

    def body(r, carry):
        rows = pl.ds(pl.multiple_of(r * seg, seg), seg)
        oa_ref[rows, :] = _conv_a_ln(ua_ref[rows, :], pos, seg, wa_ref, ba_ref, lg_ref, lbeta_ref).astype(BF16)
        up, ok_up = grid_row(cc_ref, r, -1)
        dn, ok_dn = grid_row(cc_ref, r, 1)
        u = up * (wc_ref[0:1, :] * ok_up) + cc_ref[rows, :] * wc_ref[1:2, :] + dn * (wc_ref[2:3, :] * ok_dn)
        oc_ref[rows, :] = (cb_ref[rows, :] * u).astype(BF16)
        acc = jnp.zeros((seg, W_BR), F32)
        cnt = jnp.zeros((1, W_BR), F32)
        for j in range(-max_half, max_half):
            lane_on = ((j >= -half) & (j <= half - 1)).astype(F32)
            x, ok = grid_row(d_ref, r, j)
            wgt = lane_on * ok
            acc = acc + x * wgt
            cnt = cnt + wgt
        od_ref[rows, :] = (acc / cnt - d_ref[rows, :]).astype(BF16)
        return carry

    lax.fori_loop(0, n_rows, body, 0)


def _mixers(z, p, latent):
    t = z.shape[0]
    blk = LAT_SEQ
    zspec = lambda col: pl.BlockSpec((blk, W_BR), lambda i, col=col: (i, col))
    full = lambda a: pl.BlockSpec(a.shape, lambda i: (0,) * a.ndim)
    params = [p["conv_a_w"], p["conv_a_b"], p["ln_a_g"], p["ln_a_b"], p["conv_c_w"]]
    out = jax.ShapeDtypeStruct((t, W_BR), BF16)
    return pl.pallas_call(
        _mix_lat_kernel if latent else _mix_ctx_kernel,
        grid=(t // blk,),
        in_specs=[zspec(0), zspec(6), zspec(7), zspec(8)] + [full(a) for a in params],
        out_specs=[pl.BlockSpec((blk, W_BR), lambda i: (i, 0))] * 3,
        out_shape=[out, out, out],
        compiler_params=_cparams(),
        name="mix_lat" if latent else "mix_ctx",
    )(z, z, z, z, *params)


def _chunk_cumsums(x, cpos):
    n = x.shape[0]
    fwd, rev = x, x
    step = 1
    while step < HG_CHUNK:
        fwd = fwd + jnp.where(cpos >= step, pltpu.roll(fwd, step, axis=0), 0.0)
        rev = rev + jnp.where(cpos < HG_CHUNK - step, pltpu.roll(rev, n - step, axis=0), 0.0)
        step *= 2
    return fwd, rev


def _hgrn_kernel(*refs, has_s0):
    if has_s0:
        (q_ref, lff_ref, lfb_ref, v_ref, sg_ref, ng_ref, hm_ref, s0_ref,
         o_ref, sfin_ref, oacc, qp, kp, tot, st) = refs
    else:
        (q_ref, lff_ref, lfb_ref, v_ref, sg_ref, ng_ref, hm_ref,
         o_ref, sfin_ref, oacc, qp, kp, tot, st) = refs
        s0_ref = None
    seq = q_ref.shape[0]
    n_tiles = seq // HG_TILE
    n_chunks = seq // HG_CHUNK
    cpos = jnp.bitwise_and(_row_index(HG_TILE), HG_CHUNK - 1)
    head_ones = hm_ref[...].astype(BF16)

    for direction, lf_ref in ((0, lff_ref), (1, lfb_ref)):
        back = direction == 1

        def tile_body(i, carry, lf_ref=lf_ref, back=back):
            rows = pl.ds(pl.multiple_of(i * HG_TILE, HG_TILE), HG_TILE)
            lf = lf_ref[rows, :]
            q = q_ref[rows, :]
            v = v_ref[rows, :]
            fwd, rev = _chunk_cumsums(lf, cpos)
            b, rest = (rev, fwd - lf) if back else (fwd, rev - lf)
            key = 1.0 - jnp.exp(lf)
            tot[rows, :] = fwd + rev - lf
            qp[rows, :] = (q * jnp.exp(b)).astype(BF16)
            kp[rows, :] = (key * jnp.exp(rest)).astype(BF16)
            acc = _dot((q * key).astype(BF16), head_ones) * v
            for dist in range(1, HG_CHUNK):
                sft = (HG_TILE - dist) if back else dist
                valid = (cpos < HG_CHUNK - dist) if back else (cpos >= dist)
                ks = pltpu.roll(key, sft, axis=0)
                bs = pltpu.roll(b, sft, axis=0)
                vs = pltpu.roll(v, sft, axis=0)
                e = jnp.where(valid, q * ks * jnp.exp(b - bs), 0.0)
                acc = acc + _dot(e.astype(BF16), head_ones) * vs
            if back:
                oacc[rows, :] = oacc[rows, :] + acc
            else:
                oacc[rows, :] = acc
            return carry

        lax.fori_loop(0, n_tiles, tile_body, 0)

        if s0_ref is None:
            st[...] = jnp.zeros_like(st)
        else:
            st[...] = s0_ref[0, direction]

        def chunk_body(ci, carry, back=back):
            c = (n_chunks - 1 - ci) if back else ci
            rows = pl.ds(pl.multiple_of(c * HG_CHUNK, HG_CHUNK), HG_CHUNK)
            s_old = st[...]
            inter = lax.dot_general(qp[rows, :], s_old.astype(BF16), (((1,), (1,)), ((), ())),
                                    preferred_element_type=F32)
            oacc[rows, :] = oacc[rows, :] + inter
            kv_t = lax.dot_general(v_ref[rows, :].astype(BF16), kp[rows, :], (((0,), (0,)), ((), ())),
                                   preferred_element_type=F32)
            decay = jnp.exp(tot[pl.ds(pl.multiple_of(c * HG_CHUNK, HG_CHUNK), 1), :])
            st[...] = s_old * decay + kv_t * hm_ref[...]
            return carry

        lax.fori_loop(0, n_chunks, chunk_body, 0)
        sfin_ref[0, direction] = st[...].T

    def out_body(i, carry):
        rows = pl.ds(pl.multiple_of(i * HG_TILE, HG_TILE), HG_TILE)
        o = oacc[rows, :]
        o2 = o * o
        hi = o2.astype(BF16)
        lo = (o2 - hi.astype(F32)).astype(BF16)
        ms = (_dot(hi, head_ones) + _dot(lo, head_ones)) * (1.0 / HGRN_HD)
        o_ref[rows, :] = (o * lax.rsqrt(ms + EPS) * ng_ref[...] * sg_ref[rows, :]).astype(BF16)
        return carry

    lax.fori_loop(0, n_tiles, out_body, 0)


def _hgrn(z, norm_g, head_mask, s0, seq):
    t = z.shape[0]
    n_seq = t // seq
    zspec = lambda col: pl.BlockSpec((seq, W_BR), lambda i, col=col: (i, col))
    wb = H_B * HGRN_HD
    in_specs = [zspec(1), zspec(2), zspec(3), zspec(4), zspec(5),
                pl.BlockSpec((1, wb), lambda i: (0, 0)),
                pl.BlockSpec((wb, wb), lambda i: (0, 0))]
    args = [z, z, z, z, z, norm_g, head_mask]
    if s0 is not None:
        in_specs.append(pl.BlockSpec((1, 2, wb, wb), lambda i: (i, 0, 0, 0)))
        args.append(s0)
    return pl.pallas_call(
        functools.partial(_hgrn_kernel, has_s0=s0 is not None),
        grid=(n_seq,),
        in_specs=in_specs,
        out_specs=[pl.BlockSpec((seq, wb), lambda i: (i, 0)),
                   pl.BlockSpec((1, 2, wb, wb), lambda i: (i, 0, 0, 0))],
        out_shape=[jax.ShapeDtypeStruct((t, wb), BF16), jax.ShapeDtypeStruct((n_seq, 2, wb, wb), F32)],
        scratch_shapes=[pltpu.VMEM((seq, wb), F32), pltpu.VMEM((seq, wb), BF16), pltpu.VMEM((seq, wb), BF16),
                        pltpu.VMEM((seq, wb), F32), pltpu.VMEM((wb, wb), F32)],
        compiler_params=_cparams(),
        name="hgrn_lat" if s0 is not None else "hgrn_ctx",
    )(*args)


def _merge_kernel(h_ref, ua_ref, ub_ref, uc_ref, ud_ref, x_ref, mod_ref, wg_ref, wa_ref, wb_ref, wc_ref, wd_ref,
                  ps_ref, wo_ref, g2_ref, x1_ref, h2_ref):
    d = D_MODEL
    h = h_ref[...]
    merged = None
    for i, (u_ref, w_ref) in enumerate(((ua_ref, wa_ref), (ub_ref, wb_ref), (uc_ref, wc_ref), (ud_ref, wd_ref))):
        y = _dot(u_ref[...], w_ref[...])
        if i == 3:
            y = y * ps_ref[...]
        term = _sigmoid(_dot(h, wg_ref[:, i * d:(i + 1) * d])) * y
        merged = term if merged is None else merged + term
    mix = _dot(merged.astype(BF16), wo_ref[...])
    mod = mod_ref[0]
    g_m, sh_f, sc_f = mod[:, 2 * d:3 * d], mod[:, 3 * d:4 * d], mod[:, 4 * d:5 * d]
    x1 = x_ref[...] + g_m * mix
    x1_ref[...] = x1
    h2_ref[...] = (_rmsnorm(x1, g2_ref[...]) * (1.0 + sc_f) + sh_f).astype(BF16)


def _merge(h, us, x, mod, w, tokens_per_mod):
    t, d = x.shape
    tm = 256
    tok = lambda width: pl.BlockSpec((tm, width), lambda i: (i, 0))
    full = lambda a: pl.BlockSpec(a.shape, lambda i: (0,) * a.ndim)
    weights = [w["w_gate"], w["w_out_a"], w["w_out_b"], w["w_out_c"], w["w_pool"], w["pool_scale"], w["w_o"],
               w["norm_ffn_g"]]
    return pl.pallas_call(
        _merge_kernel,
        grid=(t // tm,),
        in_specs=[tok(d)] + [tok(W_BR)] * 4 + [tok(d),
                  pl.BlockSpec((1, 1, 6 * d), lambda i: ((i * tm) // tokens_per_mod, 0, 0))]
                 + [full(a) for a in weights],
        out_specs=[tok(d), tok(d)],
        out_shape=[jax.ShapeDtypeStruct((t, d), F32), jax.ShapeDtypeStruct((t, d), BF16)],
        compiler_params=_cparams(),
        name="merge",
    )(h, *us, x, mod, *weights)


FFN_CHUNK = 256


def _ffn_kernel(h2_ref, x1_ref, mod_ref, w13_ref, w2_ref, gf_ref, o_ref, *, final):
    d = D_MODEL
    h2 = h2_ref[...]
    acc = None
    for j in range(D_FF // FFN_CHUNK):
        lo, hi = j * FFN_CHUNK, (j + 1) * FFN_CHUNK
        gate = _dot(h2, w13_ref[:, lo:hi])
        up = _dot(h2, w13_ref[:, D_FF + lo:D_FF + hi])
        part = _dot((_silu(gate) * up).astype(BF16), w2_ref[lo:hi, :])
        acc = part if acc is None else acc + part
    g_f = mod_ref[0][:, 5 * d:6 * d]
    x2 = x1_ref[...] + g_f * acc
    if final:
        x2 = _rmsnorm(x2, gf_ref[...])
    o_ref[...] = x2


def _ffn(h2, x1, mod, w13, w2, final_g, tokens_per_mod, final):
    t, d = x1.shape
    tm = 256
    full = lambda a: pl.BlockSpec(a.shape, lambda i: (0,) * a.ndim)
    return pl.pallas_call(
        functools.partial(_ffn_kernel, final=final),
        grid=(t // tm,),
        in_specs=[pl.BlockSpec((tm, d), lambda i: (i, 0)), pl.BlockSpec((tm, d), lambda i: (i, 0)),
                  pl.BlockSpec((1, 1, 6 * d), lambda i: ((i * tm) // tokens_per_mod, 0, 0)),
                  full(w13), full(w2), full(final_g)],
        out_specs=pl.BlockSpec((tm, d), lambda i: (i, 0)),
        out_shape=jax.ShapeDtypeStruct((t, d), F32),
        compiler_params=_cparams(),
        name="ffn",
    )(h2, x1, mod, w13, w2, final_g)


def _block_diag_heads(s):
    eye = jnp.eye(H_B, dtype=s.dtype)
    bd = jnp.einsum("...hab,hg->...hagb", s, eye)
    return bd.reshape(s.shape[:-3] + (H_B * s.shape[-2], H_B * s.shape[-1]))


def kernel(x_prompt, x_sample, state_hgrn, c, c_ctx, ada_w, ada_b, norm_mix_g, w_in, conv_a_w, conv_a_b, ln_a_g,
           ln_a_b, w_out_a, hgrn_lb_logits, hgrn_norm_g, w_out_b, conv_c_w, w_out_c, pool_w, pool_scale, w_o,
           norm_ffn_g, ffn_w13, ffn_w2, final_norm_g):
    depth = w_in.shape[0]
    d = D_MODEL
    n_ctx, n_lat = x_prompt.shape[0], x_sample.shape[0]
    t_ctx, t_lat = n_ctx * CTX_SEQ, n_lat * LAT_SEQ

    n_rows = -(-(1 + n_lat) // 8) * 8
    c_rows = jnp.zeros((n_rows, d), F32).at[0].set(c_ctx).at[1:1 + n_lat].set(c)
    mod = _modulation(c_rows, ada_w, ada_b)

    lb_all = jnp.cumsum(jax.nn.softmax(hgrn_lb_logits.astype(F32), axis=0), axis=0)
    lb_all = (lb_all - lb_all[:1]).reshape(depth, 1, 2 * W_BR)

    head_mask = _block_diag_heads(jnp.ones((H_B, HGRN_HD, HGRN_HD), F32))
    s0_lat = _block_diag_heads(jnp.swapaxes(state_hgrn, -1, -2))

    row = lambda a: a.reshape(1, -1)
    y_p = x_prompt.reshape(t_ctx, d)
    y_s = x_sample.reshape(t_lat, d)
    ctx_states = []
    for l in range(depth):
        w_in_b = w_in[l].astype(BF16)
        w = {
            "w_mix": w_in_b[:, :N_MIX], "w_gate": w_in_b[:, N_MIX:],
            "w_out_a": w_out_a[l].astype(BF16), "w_out_b": w_out_b[l].astype(BF16),
            "w_out_c": w_out_c[l].astype(BF16),
            "w_pool": _block_diag_heads(pool_w[l]).astype(BF16),
            "pool_scale": row(pool_scale[l]), "w_o": w_o[l].astype(BF16), "norm_ffn_g": row(norm_ffn_g[l]),
            "conv_a_w": conv_a_w[l], "conv_a_b": row(conv_a_b[l]), "ln_a_g": row(ln_a_g[l]),
            "ln_a_b": row(ln_a_b[l]), "conv_c_w": conv_c_w[l],
        }
        w13 = ffn_w13[l].astype(BF16)
        w2 = ffn_w2[l].astype(BF16)
        final = l == depth - 1
        outs = []
        for latent, x, tokens_per_mod in ((False, y_p, t_ctx), (True, y_s, LAT_SEQ)):
            m = (mod[l, 1:1 + n_lat] if latent else mod[l, 0:1])[:, None, :]
            h, z = _inproj(x, m, row(norm_mix_g[l]), w["w_mix"], lb_all[l], tokens_per_mod)
            u_a, u_c, u_d = _mixers(z, w, latent)
            u_b, s_fin = _hgrn(z, row(hgrn_norm_g[l]), head_mask, s0_lat[:, l] if latent else None,
                               LAT_SEQ if latent else CTX_SEQ)
            x1, h2 = _merge(h, (u_a, u_b, u_c, u_d), x, m, w, tokens_per_mod)
            outs.append(_ffn(h2, x1, m, w13, w2, row(final_norm_g), tokens_per_mod, final))
            if not latent:
                blocks = s_fin.reshape(n_ctx, 2, H_B, HGRN_HD, H_B, HGRN_HD)
                ctx_states.append(jnp.stack([blocks[:, :, hh, :, hh, :] for hh in range(H_B)], axis=2))
        y_p, y_s = outs

    new_state = jnp.stack(ctx_states, axis=1).astype(x_prompt.dtype)
    return (y_p.reshape(x_prompt.shape), y_s.reshape(x_sample.shape), new_state)
```

```python
import functools

import jax
import jax.numpy as jnp
from jax import lax
from jax.experimental import pallas as pl
from jax.experimental.pallas import tpu as pltpu

F32 = jnp.float32
BF16 = jnp.bfloat16

D_MODEL = 1024
GRID_W = 64
EPS = 1e-6
W_BR = 256
CONV_A_K = 31
H_B = 4
HGRN_HD = 64
POOL_WINDOWS = (2, 4, 8, 16)
POOL_GC = 64
D_FF = 2816
N_MIX = 11 * W_BR
N_Z = 9 * W_BR
CTX_SEQ = 256
LAT_SEQ = 2048

HG_CHUNK = 16
HG_TILE = 128
VMEM_LIMIT = 56 * 1024 * 1024


def _cparams(n_axes=1):
    return pltpu.CompilerParams(dimension_semantics=("arbitrary",) * n_axes, vmem_limit_bytes=VMEM_LIMIT)


def _sigmoid(x):
    return 1.0 / (1.0 + jnp.exp(-x))


def _silu(x):
    return x * _sigmoid(x)


def _rmsnorm(x, g):
    return x * lax.rsqrt(jnp.mean(x * x, axis=-1, keepdims=True) + EPS) * g


def _dot(a, b):
    return jnp.dot(a, b, preferred_element_type=F32)


def _mod_kernel(c_ref, w_ref, b_ref, o_ref):
    s = _silu(c_ref[...]).astype(BF16)
    o_ref[0] = _dot(s, w_ref[0].astype(BF16)) + b_ref[0]


def _modulation(c_rows, ada_w, ada_b):
    depth, d, n = ada_w.shape
    nr = c_rows.shape[0]
    tn = 768
    return pl.pallas_call(
        _mod_kernel,
        grid=(depth, n // tn),
        in_specs=[
            pl.BlockSpec((nr, d), lambda l, j: (0, 0)),
            pl.BlockSpec((1, d, tn), lambda l, j: (l, 0, j)),
            pl.BlockSpec((1, 1, tn), lambda l, j: (l, 0, j)),
        ],
        out_specs=pl.BlockSpec((1, nr, tn), lambda l, j: (l, 0, j)),
        out_shape=jax.ShapeDtypeStruct((depth, nr, n), F32),
        compiler_params=_cparams(2),
        name="modulation",
    )(c_rows, ada_w, ada_b.reshape(depth, 1, n))


def _log_forget(z, lb):
    ls = jnp.minimum(z, 0.0) - jnp.log1p(jnp.exp(-jnp.abs(z)))
    a = jnp.log(lb)
    b = jnp.log1p(-lb) + ls
    m = jnp.maximum(a, b)
    return m + jnp.log(jnp.exp(a - m) + jnp.exp(b - m))


def _inproj_kernel(x_ref, mod_ref, g_ref, w_ref, lb_ref, h_ref, z_ref):
    d, w = D_MODEL, W_BR
    mod = mod_ref[0]
    sh, sc = mod[:, 0:d], mod[:, d:2 * d]
    h = (_rmsnorm(x_ref[...], g_ref[...]) * (1.0 + sc) + sh).astype(BF16)
    h_ref[...] = h

    def proj(j):
        return _dot(h, w_ref[:, j * w:(j + 1) * w])

    lb = lb_ref[...]
    z_ref[:, 0 * w:1 * w] = proj(0) * _sigmoid(proj(1))
    z_ref[:, 1 * w:2 * w] = proj(2)
    z_ref[:, 2 * w:3 * w] = _log_forget(proj(3), lb[:, 0:w])
    z_ref[:, 3 * w:4 * w] = _log_forget(proj(4), lb[:, w:2 * w])
    z_ref[:, 4 * w:5 * w] = proj(5)
    z_ref[:, 5 * w:6 * w] = _silu(proj(6))
    z_ref[:, 6 * w:7 * w] = proj(7)
    z_ref[:, 7 * w:8 * w] = proj(8) * proj(9)
    z_ref[:, 8 * w:9 * w] = proj(10)


def _inproj(x, mod, g, w_mix, lb, tokens_per_mod):
    t, d = x.shape
    tm = 512
    return pl.pallas_call(
        _inproj_kernel,
        grid=(t // tm,),
        in_specs=[
            pl.BlockSpec((tm, d), lambda i: (i, 0)),
            pl.BlockSpec((1, 1, 6 * d), lambda i: ((i * tm) // tokens_per_mod, 0, 0)),
            pl.BlockSpec((1, d), lambda i: (0, 0)),
            pl.BlockSpec((d, N_MIX), lambda i: (0, 0)),
            pl.BlockSpec((1, 2 * W_BR), lambda i: (0, 0)),
        ],
        out_specs=[
            pl.BlockSpec((tm, d), lambda i: (i, 0)),
            pl.BlockSpec((tm, N_Z), lambda i: (i, 0)),
        ],
        out_shape=[jax.ShapeDtypeStruct((t, d), BF16), jax.ShapeDtypeStruct((t, N_Z), F32)],
        compiler_params=_cparams(),
        name="inproj",
    )(x, mod, g, w_mix, lb)


def _row_index(rows):
    return lax.broadcasted_iota(jnp.int32, (rows, W_BR), 0)


def _shift_rows(x, j, pos, seg):
    if j == 0:
        return x
    y = pltpu.roll(x, (-j) % x.shape[0], axis=0)
    valid = (pos < seg - j) if j > 0 else (pos >= -j)
    return jnp.where(valid, y, 0.0)


def _conv_a_ln(x, pos, seg, wa_ref, ba_ref, lg_ref, lbeta_ref):
    half = CONV_A_K // 2
    acc = x * wa_ref[half:half + 1, :]
    for j in range(-half, half + 1):
        if j != 0:
            acc = acc + _shift_rows(x, j, pos, seg) * wa_ref[half + j:half + j + 1, :]
    u = acc + ba_ref[...]
    mu = jnp.mean(u, axis=-1, keepdims=True)
    var = jnp.mean(jnp.square(u - mu), axis=-1, keepdims=True)
    return _silu((u - mu) * lax.rsqrt(var + EPS) * lg_ref[...] + lbeta_ref[...])


def _pool_lane_half():
    lane = lax.broadcasted_iota(jnp.int32, (1, W_BR), 1)
    return jnp.left_shift(1, lane // POOL_GC)


def _mix_ctx_kernel(ua_ref, cb_ref, cc_ref, d_ref, wa_ref, ba_ref, lg_ref, lbeta_ref, wc_ref,
                    oa_ref, oc_ref, od_ref):
    seg = CTX_SEQ
    pos = _row_index(seg)
    half = _pool_lane_half()
    max_half = max(POOL_WINDOWS) // 2

    def body(s, carry):
        rows = pl.ds(pl.multiple_of(s * seg, seg), seg)
        oa_ref[rows, :] = _conv_a_ln(ua_ref[rows, :], pos, seg, wa_ref, ba_ref, lg_ref, lbeta_ref).astype(BF16)
        cc = cc_ref[rows, :]
        u = (_shift_rows(cc, -1, pos, seg) * wc_ref[0:1, :] + cc * wc_ref[1:2, :]
             + _shift_rows(cc, 1, pos, seg) * wc_ref[2:3, :])
        oc_ref[rows, :] = (cb_ref[rows, :] * u).astype(BF16)
        dd = d_ref[rows, :]
        acc = jnp.zeros_like(dd)
        cnt = jnp.zeros_like(dd)
        for j in range(-max_half, max_half):
            lane_on = (j >= -half) & (j <= half - 1)
            valid = (pos + j >= 0) & (pos + j < seg) & lane_on
            acc = acc + jnp.where(valid, _shift_rows(dd, j, pos, seg), 0.0)
            cnt = cnt + jnp.where(valid, 1.0, 0.0)
        od_ref[rows, :] = (acc / cnt - dd).astype(BF16)
        return carry

    lax.fori_loop(0, ua_ref.shape[0] // seg, body, 0)


def _mix_lat_kernel(ua_ref, cb_ref, cc_ref, d_ref, wa_ref, ba_ref, lg_ref, lbeta_ref, wc_ref,
                    oa_ref, oc_ref, od_ref):
    seg = GRID_W
    n_rows = LAT_SEQ // GRID_W
    pos = _row_index(seg)
    half = _pool_lane_half()
    max_half = max(POOL_WINDOWS) // 2

    def grid_row(ref, r, j):
        rr = jnp.clip(r + j, 0, n_rows - 1)
        ok = jnp.logical_and(r + j >= 0, r + j < n_rows).astype(F32)
        return ref[pl.ds(pl.multiple_of(rr * seg, seg), seg), :], ok

    def body(r, carry):
        rows = pl.ds(pl.multiple_of(r * seg, seg), seg)
        oa_ref[rows, :] = _conv_a_ln(ua_ref[rows, :], pos, seg, wa_ref, ba_ref, lg_ref, lbeta_ref).astype(BF16)
        up, ok_up = grid_row(cc_ref, r, -1)
        dn, ok_dn = grid_row(cc_ref, r, 1)
        u = up * (wc_ref[0:1, :] * ok_up) + cc_ref[rows, :] * wc_ref[1:2, :] + dn * (wc_ref[2:3, :] * ok_dn)
        oc_ref[rows, :] = (cb_ref[rows, :] * u).astype(BF16)
        acc = jnp.zeros((seg, W_BR), F32)
        cnt = jnp.zeros((1, W_BR), F32)
        for j in range(-max_half, max_half):
            lane_on = ((j >= -half) & (j <= half - 1)).astype(F32)
            x, ok = grid_row(d_ref, r, j)
            wgt = lane_on * ok
            acc = acc + x * wgt
            cnt = cnt + wgt
        od_ref[rows, :] = (acc / cnt - d_ref[rows, :]).astype(BF16)
        return carry

    lax.fori_loop(0, n_rows, body, 0)


def _mixers(z, p, latent):
    t = z.shape[0]
    blk = LAT_SEQ
    zspec = lambda col: pl.BlockSpec((blk, W_BR), lambda i, col=col: (i, col))
    full = lambda a: pl.BlockSpec(a.shape, lambda i: (0,) * a.ndim)
    params = [p["conv_a_w"], p["conv_a_b"], p["ln_a_g"], p["ln_a_b"], p["conv_c_w"]]
    out = jax.ShapeDtypeStruct((t, W_BR), BF16)
    return pl.pallas_call(
        _mix_lat_kernel if latent else _mix_ctx_kernel,
        grid=(t // blk,),
        in_specs=[zspec(0), zspec(6), zspec(7), zspec(8)] + [full(a) for a in params],
        out_specs=[pl.BlockSpec((blk, W_BR), lambda i: (i, 0))] * 3,
        out_shape=[out, out, out],
        compiler_params=_cparams(),
        name="mix_lat" if latent else "mix_ctx",
    )(z, z, z, z, *params)


def _chunk_cumsums(x, cpos):
    n = x.shape[0]
    fwd, rev = x, x
    step = 1
    while step < HG_CHUNK:
        fwd = fwd + jnp.where(cpos >= step, pltpu.roll(fwd, step, axis=0), 0.0)
        rev = rev + jnp.where(cpos < HG_CHUNK - step, pltpu.roll(rev, n - step, axis=0), 0.0)
        step *= 2
    return fwd, rev


def _hgrn_kernel(*refs, has_s0):
    if has_s0:
        (q_ref, lff_ref, lfb_ref, v_ref, sg_ref, ng_ref, hm_ref, s0_ref,
         o_ref, sfin_ref, oacc, qp, kp, tot, st) = refs
    else:
        (q_ref, lff_ref, lfb_ref, v_ref, sg_ref, ng_ref, hm_ref,
         o_ref, sfin_ref, oacc, qp, kp, tot, st) = refs
        s0_ref = None
    seq = q_ref.shape[0]
    n_tiles = seq // HG_TILE
    n_chunks = seq // HG_CHUNK
    cpos = jnp.bitwise_and(_row_index(HG_TILE), HG_CHUNK - 1)
    head_ones = hm_ref[...].astype(BF16)

    for direction, lf_ref in ((0, lff_ref), (1, lfb_ref)):
        back = direction == 1

        def tile_body(i, carry, lf_ref=lf_ref, back=back):
            rows = pl.ds(pl.multiple_of(i * HG_TILE, HG_TILE), HG_TILE)
            lf = lf_ref[rows, :]
            q = q_ref[rows, :]
            v = v_ref[rows, :]
            fwd, rev = _chunk_cumsums(lf, cpos)
            b, rest = (rev, fwd - lf) if back else (fwd, rev - lf)
            key = 1.0 - jnp.exp(lf)
            tot[rows, :] = fwd + rev - lf
            qp[rows, :] = (q * jnp.exp(b)).astype(BF16)
            kp[rows, :] = (key * jnp.exp(rest)).astype(BF16)
            acc = _dot((q * key).astype(BF16), head_ones) * v
            for dist in range(1, HG_CHUNK):
                sft = (HG_TILE - dist) if back else dist
                valid = (cpos < HG_CHUNK - dist) if back else (cpos >= dist)
                ks = pltpu.roll(key, sft, axis=0)
                bs = pltpu.roll(b, sft, axis=0)
                vs = pltpu.roll(v, sft, axis=0)
                e = jnp.where(valid, q * ks * jnp.exp(b - bs), 0.0)
                acc = acc + _dot(e.astype(BF16), head_ones) * vs
            if back:
                oacc[rows, :] = oacc[rows, :] + acc
            else:
                oacc[rows, :] = acc
            return carry

        lax.fori_loop(0, n_tiles, tile_body, 0)

        if s0_ref is None:
            st[...] = jnp.zeros_like(st)
        else:
            st[...] = s0_ref[0, direction]

        def chunk_body(ci, carry, back=back):
            c = (n_chunks - 1 - ci) if back else ci
            rows = pl.ds(pl.multiple_of(c * HG_CHUNK, HG_CHUNK), HG_CHUNK)
            s_old = st[...]
            inter = lax.dot_general(qp[rows, :], s_old.astype(BF16), (((1,), (1,)), ((), ())),
                                    preferred_element_type=F32)
            oacc[rows, :] = oacc[rows, :] + inter
            kv_t = lax.dot_general(v_ref[rows, :].astype(BF16), kp[rows, :], (((0,), (0,)), ((), ())),
                                   preferred_element_type=F32)
            decay = jnp.exp(tot[pl.ds(pl.multiple_of(c * HG_CHUNK, HG_CHUNK), 1), :])
            st[...] = s_old * decay + kv_t * hm_ref[...]
            return carry

        lax.fori_loop(0, n_chunks, chunk_body, 0)
        sfin_ref[0, direction] = st[...].T

    def out_body(i, carry):
        rows = pl.ds(pl.multiple_of(i * HG_TILE, HG_TILE), HG_TILE)
        o = oacc[rows, :]
        o2 = o * o
        hi = o2.astype(BF16)
        lo = (o2 - hi.astype(F32)).astype(BF16)
        ms = (_dot(hi, head_ones) + _dot(lo, head_ones)) * (1.0 / HGRN_HD)
        o_ref[rows, :] = (o * lax.rsqrt(ms + EPS) * ng_ref[...] * sg_ref[rows, :]).astype(BF16)
        return carry

    lax.fori_loop(0, n_tiles, out_body, 0)


def _hgrn(z, norm_g, head_mask, s0, seq):
    t = z.shape[0]
    n_seq = t // seq
    zspec = lambda col: pl.BlockSpec((seq, W_BR), lambda i, col=col: (i, col))
    wb = H_B * HGRN_HD
    in_specs = [zspec(1), zspec(2), zspec(3), zspec(4), zspec(5),
                pl.BlockSpec((1, wb), lambda i: (0, 0)),
                pl.BlockSpec((wb, wb), lambda i: (0, 0))]
    args = [z, z, z, z, z, norm_g, head_mask]
    if s0 is not None:
        in_specs.append(pl.BlockSpec((1, 2, wb, wb), lambda i: (i, 0, 0, 0)))
        args.append(s0)
    return pl.pallas_call(
        functools.partial(_hgrn_kernel, has_s0=s0 is not None),
        grid=(n_seq,),
        in_specs=in_specs,
        out_specs=[pl.BlockSpec((seq, wb), lambda i: (i, 0)),
                   pl.BlockSpec((1, 2, wb, wb), lambda i: (i, 0, 0, 0))],
        out_shape=[jax.ShapeDtypeStruct((t, wb), BF16), jax.ShapeDtypeStruct((n_seq, 2, wb, wb), F32)],
        scratch_shapes=[pltpu.VMEM((seq, wb), F32), pltpu.VMEM((seq, wb), BF16), pltpu.VMEM((seq, wb), BF16),
                        pltpu.VMEM((seq, wb), F32), pltpu.VMEM((wb, wb), F32)],
        compiler_params=_cparams(),
        name="hgrn_lat" if s0 is not None else "hgrn_ctx",
    )(*args)


SB = 256
SUB = 8
LEVELS = (8, 16, 32, 64, 128)
NQ = SB // 2


def _level_spans(m, upper):
    off = m if upper else 0
    return [(i * 2 * m + off, i * 2 * m + off + m) for i in range(SB // (2 * m))]


def _take(x, spans):
    parts = [x[a:b] for a, b in spans]
    return parts[0] if len(parts) == 1 else jnp.concatenate(parts, axis=0)


def _level_masks():
    import numpy as np
    tq = np.arange(4 * NQ)[:, None] % NQ
    ts = np.arange(NQ)[None, :]
    return np.stack([((tq // m) == (ts // m)) for m in LEVELS[:-1]]).astype(np.float32)


def _cumsum_rows(x):
    g = SB // SUB
    x3 = x.reshape(g, SUB, x.shape[-1])
    sub = lax.broadcasted_iota(jnp.int32, x3.shape, 1)
    step = 1
    while step < SUB:
        x3 = x3 + jnp.where(sub >= step, pltpu.roll(x3, step, axis=1), 0.0)
        step *= 2
    tot = x3[:, SUB - 1:SUB, :]
    offs = [jnp.zeros_like(tot[0:1])]
    for i in range(1, g):
        offs.append(offs[-1] + tot[i - 1:i])
    return (x3 + jnp.concatenate(offs, axis=0)).reshape(x.shape)


def _hgrn_sb_kernel(*refs, has_s0):
    if has_s0:
        (q_ref, lff_ref, lfb_ref, v_ref, sg_ref, ng_ref, hm_ref, lm_ref, s0_ref,
         o_ref, sfin_ref, oacc, st) = refs
    else:
        (q_ref, lff_ref, lfb_ref, v_ref, sg_ref, ng_ref, hm_ref, lm_ref,
         o_ref, sfin_ref, oacc, st) = refs
        s0_ref = None
    seq, w = q_ref.shape
    n_sb = seq // SB
    with_inter = has_s0 or n_sb > 1
    head_ones = hm_ref[...].astype(BF16)
    lane_head = lax.broadcasted_iota(jnp.int32, (1, w), 1) // HGRN_HD
    in_head = [lane_head == h for h in range(H_B)]
    sub3 = lax.broadcasted_iota(jnp.int32, (SB // SUB, SUB, w), 1)
    nt = (((1,), (1,)), ((), ()))
    tn = (((0,), (0,)), ((), ()))

    for direction, lf_ref in ((0, lff_ref), (1, lfb_ref)):
        back = direction == 1
        if s0_ref is None:
            st[...] = jnp.zeros_like(st)
        else:
            st[...] = s0_ref[0, direction]

        def sb_body(si, carry, lf_ref=lf_ref, back=back):
            base = pl.multiple_of(((n_sb - 1 - si) if back else si) * SB, SB)
            rows = pl.ds(base, SB)
            lf = lf_ref[rows, :]
            q = q_ref[rows, :]
            v = v_ref[rows, :]
            vb = v.astype(BF16)
            key = 1.0 - jnp.exp(lf)
            g = _cumsum_rows(lf)
            tot = g[SB - 1:SB, :]
            b = (lf - g) if back else g

            q3, k3, b3, v3 = (x.reshape(SB // SUB, SUB, w) for x in (q, key, b, v))
            acc = _dot((q * key).astype(BF16), head_ones) * v
            for dist in range(1, SUB):
                sft = (SUB - dist) if back else dist
                valid = (sub3 < SUB - dist) if back else (sub3 >= dist)
                e = jnp.where(valid, q3 * pltpu.roll(k3, sft, axis=1) * jnp.exp(b3 - pltpu.roll(b3, sft, axis=1)), 0.0)
                red = _dot(e.reshape(SB, w).astype(BF16), head_ones)
                acc = acc + red * pltpu.roll(v3, sft, axis=1).reshape(SB, w)
            if back:
                oacc[rows, :] = oacc[rows, :] + acc
            else:
                oacc[rows, :] = acc

            for li, m in enumerate(LEVELS):
                lower, upper = _level_spans(m, False), _level_spans(m, True)
                qs, ks = (lower, upper) if back else (upper, lower)
                edge = [(a + m) if back else (a + m - 1) for a, _ in lower]
                bm = jnp.concatenate([jnp.broadcast_to(b[r:r + 1], (m, w)) for r in edge], axis=0)
                qg = _take(q, qs) * jnp.exp(_take(b, qs) - bm)
                kg = (_take(key, ks) * jnp.exp(bm - _take(b, ks))).astype(BF16)
                stacked = jnp.concatenate([jnp.where(in_head[h], qg, 0.0) for h in range(H_B)], axis=0)
                p = lax.dot_general(stacked.astype(BF16), kg, nt, preferred_element_type=F32)
                if m < NQ:
                    p = p * lm_ref[li]
                r = _dot(p.astype(BF16), _take(vb, ks))
                res = jnp.where(in_head[0], r[0:NQ], 0.0)
                for h in range(1, H_B):
                    res = res + jnp.where(in_head[h], r[h * NQ:(h + 1) * NQ], 0.0)
                for bi, (a0, _) in enumerate(qs):
                    dst = pl.ds(base + a0, m)
                    oacc[dst, :] = oacc[dst, :] + res[bi * m:(bi + 1) * m]

            cq, ck = (tot, 0.0) if back else (0.0, tot)
            s_old = st[...]
            if with_inter:
                qp = (q * jnp.exp(b + cq)).astype(BF16)
                oacc[rows, :] = oacc[rows, :] + lax.dot_general(qp, s_old.astype(BF16), nt,
                                                                preferred_element_type=F32)
            kp = (key * jnp.exp(ck - b)).astype(BF16)
            kv_t = lax.dot_general(vb, kp, tn, preferred_element_type=F32)
            st[...] = s_old * jnp.exp(tot) + kv_t * hm_ref[...]
            return carry

        lax.fori_loop(0, n_sb, sb_body, 0)
        sfin_ref[0, direction] = st[...].T

    def out_body(i, carry):
        rows = pl.ds(pl.multiple_of(i * SB, SB), SB)
        o = oacc[rows, :]
        o2 = o * o
        hi = o2.astype(BF16)
        lo = (o2 - hi.astype(F32)).astype(BF16)
        ms = (_dot(hi, head_ones) + _dot(lo, head_ones)) * (1.0 / HGRN_HD)
        o_ref[rows, :] = (o * lax.rsqrt(ms + EPS) * ng_ref[...] * sg_ref[rows, :]).astype(BF16)
        return carry

    lax.fori_loop(0, n_sb, out_body, 0)


def _hgrn_sb(z, norm_g, head_mask, level_masks, s0, seq):
    t = z.shape[0]
    n_seq = t // seq
    zspec = lambda col: pl.BlockSpec((seq, W_BR), lambda i, col=col: (i, col))
    wb = H_B * HGRN_HD
    in_specs = [zspec(1), zspec(2), zspec(3), zspec(4), zspec(5),
                pl.BlockSpec((1, wb), lambda i: (0, 0)),
                pl.BlockSpec((wb, wb), lambda i: (0, 0)),
                pl.BlockSpec(level_masks.shape, lambda i: (0, 0, 0))]
    args = [z, z, z, z, z, norm_g, head_mask, level_masks]
    if s0 is not None:
        in_specs.append(pl.BlockSpec((1, 2, wb, wb), lambda i: (i, 0, 0, 0)))
        args.append(s0)
    return pl.pallas_call(
        functools.partial(_hgrn_sb_kernel, has_s0=s0 is not None),
        grid=(n_seq,),
        in_specs=in_specs,
        out_specs=[pl.BlockSpec((seq, wb), lambda i: (i, 0)),
                   pl.BlockSpec((1, 2, wb, wb), lambda i: (i, 0, 0, 0))],
        out_shape=[jax.ShapeDtypeStruct((t, wb), BF16), jax.ShapeDtypeStruct((n_seq, 2, wb, wb), F32)],
        scratch_shapes=[pltpu.VMEM((seq, wb), F32), pltpu.VMEM((wb, wb), F32)],
        compiler_params=_cparams(),
        name="hgrn_lat" if s0 is not None else "hgrn_ctx",
    )(*args)


def _merge_kernel(h_ref, ua_ref, ub_ref, uc_ref, ud_ref, x_ref, mod_ref, wg_ref, wa_ref, wb_ref, wc_ref, wd_ref,
                  ps_ref, wo_ref, g2_ref, x1_ref, h2_ref):
    d = D_MODEL
    h = h_ref[...]
    merged = None
    for i, (u_ref, w_ref) in enumerate(((ua_ref, wa_ref), (ub_ref, wb_ref), (uc_ref, wc_ref), (ud_ref, wd_ref))):
        y = _dot(u_ref[...], w_ref[...])
        if i == 3:
            y = y * ps_ref[...]
        term = _sigmoid(_dot(h, wg_ref[:, i * d:(i + 1) * d])) * y
        merged = term if merged is None else merged + term
    mix = _dot(merged.astype(BF16), wo_ref[...])
    mod = mod_ref[0]
    g_m, sh_f, sc_f = mod[:, 2 * d:3 * d], mod[:, 3 * d:4 * d], mod[:, 4 * d:5 * d]
    x1 = x_ref[...] + g_m * mix
    x1_ref[...] = x1
    h2_ref[...] = (_rmsnorm(x1, g2_ref[...]) * (1.0 + sc_f) + sh_f).astype(BF16)


def _merge(h, us, x, mod, w, tokens_per_mod):
    t, d = x.shape
    tm = 256
    tok = lambda width: pl.BlockSpec((tm, width), lambda i: (i, 0))
    full = lambda a: pl.BlockSpec(a.shape, lambda i: (0,) * a.ndim)
    weights = [w["w_gate"], w["w_out_a"], w["w_out_b"], w["w_out_c"], w["w_pool"], w["pool_scale"], w["w_o"],
               w["norm_ffn_g"]]
    return pl.pallas_call(
        _merge_kernel,
        grid=(t // tm,),
        in_specs=[tok(d)] + [tok(W_BR)] * 4 + [tok(d),
                  pl.BlockSpec((1, 1, 6 * d), lambda i: ((i * tm) // tokens_per_mod, 0, 0))]
                 + [full(a) for a in weights],
        out_specs=[tok(d), tok(d)],
        out_shape=[jax.ShapeDtypeStruct((t, d), F32), jax.ShapeDtypeStruct((t, d), BF16)],
        compiler_params=_cparams(),
        name="merge",
    )(h, *us, x, mod, *weights)


FFN_CHUNK = 256


def _ffn_kernel(h2_ref, x1_ref, mod_ref, w13_ref, w2_ref, gf_ref, o_ref, *, final):
    d = D_MODEL
    h2 = h2_ref[...]
    acc = None
    for j in range(D_FF // FFN_CHUNK):
        lo, hi = j * FFN_CHUNK, (j + 1) * FFN_CHUNK
        gate = _dot(h2, w13_ref[:, lo:hi])
        up = _dot(h2, w13_ref[:, D_FF + lo:D_FF + hi])
        part = _dot((_silu(gate) * up).astype(BF16), w2_ref[lo:hi, :])
        acc = part if acc is None else acc + part
    g_f = mod_ref[0][:, 5 * d:6 * d]
    x2 = x1_ref[...] + g_f * acc
    if final:
        x2 = _rmsnorm(x2, gf_ref[...])
    o_ref[...] = x2


def _ffn(h2, x1, mod, w13, w2, final_g, tokens_per_mod, final):
    t, d = x1.shape
    tm = 256
    full = lambda a: pl.BlockSpec(a.shape, lambda i: (0,) * a.ndim)
    return pl.pallas_call(
        functools.partial(_ffn_kernel, final=final),
        grid=(t // tm,),
        in_specs=[pl.BlockSpec((tm, d), lambda i: (i, 0)), pl.BlockSpec((tm, d), lambda i: (i, 0)),
                  pl.BlockSpec((1, 1, 6 * d), lambda i: ((i * tm) // tokens_per_mod, 0, 0)),
                  full(w13), full(w2), full(final_g)],
        out_specs=pl.BlockSpec((tm, d), lambda i: (i, 0)),
        out_shape=jax.ShapeDtypeStruct((t, d), F32),
        compiler_params=_cparams(),
        name="ffn",
    )(h2, x1, mod, w13, w2, final_g)


def _block_diag_heads(s):
    eye = jnp.eye(H_B, dtype=s.dtype)
    bd = jnp.einsum("...hab,hg->...hagb", s, eye)
    return bd.reshape(s.shape[:-3] + (H_B * s.shape[-2], H_B * s.shape[-1]))


def kernel(x_prompt, x_sample, state_hgrn, c, c_ctx, ada_w, ada_b, norm_mix_g, w_in, conv_a_w, conv_a_b, ln_a_g,
           ln_a_b, w_out_a, hgrn_lb_logits, hgrn_norm_g, w_out_b, conv_c_w, w_out_c, pool_w, pool_scale, w_o,
           norm_ffn_g, ffn_w13, ffn_w2, final_norm_g):
    depth = w_in.shape[0]
    d = D_MODEL
    n_ctx, n_lat = x_prompt.shape[0], x_sample.shape[0]
    t_ctx, t_lat = n_ctx * CTX_SEQ, n_lat * LAT_SEQ

    n_rows = -(-(1 + n_lat) // 8) * 8
    c_rows = jnp.zeros((n_rows, d), F32).at[0].set(c_ctx).at[1:1 + n_lat].set(c)
    mod = _modulation(c_rows, ada_w, ada_b)

    lb_all = jnp.cumsum(jax.nn.softmax(hgrn_lb_logits.astype(F32), axis=0), axis=0)
    lb_all = (lb_all - lb_all[:1]).reshape(depth, 1, 2 * W_BR)

    head_mask = _block_diag_heads(jnp.ones((H_B, HGRN_HD, HGRN_HD), F32))
    level_masks = jnp.asarray(_level_masks())
    s0_lat = _block_diag_heads(jnp.swapaxes(state_hgrn, -1, -2))

    row = lambda a: a.reshape(1, -1)
    y_p = x_prompt.reshape(t_ctx, d)
    y_s = x_sample.reshape(t_lat, d)
    ctx_states = []
    for l in range(depth):
        w_in_b = w_in[l].astype(BF16)
        w = {
            "w_mix": w_in_b[:, :N_MIX], "w_gate": w_in_b[:, N_MIX:],
            "w_out_a": w_out_a[l].astype(BF16), "w_out_b": w_out_b[l].astype(BF16),
            "w_out_c": w_out_c[l].astype(BF16),
            "w_pool": _block_diag_heads(pool_w[l]).astype(BF16),
            "pool_scale": row(pool_scale[l]), "w_o": w_o[l].astype(BF16), "norm_ffn_g": row(norm_ffn_g[l]),
            "conv_a_w": conv_a_w[l], "conv_a_b": row(conv_a_b[l]), "ln_a_g": row(ln_a_g[l]),
            "ln_a_b": row(ln_a_b[l]), "conv_c_w": conv_c_w[l],
        }
        w13 = ffn_w13[l].astype(BF16)
        w2 = ffn_w2[l].astype(BF16)
        final = l == depth - 1
        outs = []
        for latent, x, tokens_per_mod in ((False, y_p, t_ctx), (True, y_s, LAT_SEQ)):
            m = (mod[l, 1:1 + n_lat] if latent else mod[l, 0:1])[:, None, :]
            h, z = _inproj(x, m, row(norm_mix_g[l]), w["w_mix"], lb_all[l], tokens_per_mod)
            u_a, u_c, u_d = _mixers(z, w, latent)
            u_b, s_fin = _hgrn_sb(z, row(hgrn_norm_g[l]), head_mask, level_masks,
                                  s0_lat[:, l] if latent else None, LAT_SEQ if latent else CTX_SEQ)
            x1, h2 = _merge(h, (u_a, u_b, u_c, u_d), x, m, w, tokens_per_mod)
            outs.append(_ffn(h2, x1, m, w13, w2, row(final_norm_g), tokens_per_mod, final))
            if not latent:
                blocks = s_fin.reshape(n_ctx, 2, H_B, HGRN_HD, H_B, HGRN_HD)
                ctx_states.append(jnp.stack([blocks[:, :, hh, :, hh, :] for hh in range(H_B)], axis=2))
        y_p, y_s = outs

    new_state = jnp.stack(ctx_states, axis=1).astype(x_prompt.dtype)
    return (y_p.reshape(x_prompt.shape), y_s.reshape(x_sample.shape), new_state)
```

```python
import functools

import numpy as np
import jax
import jax.numpy as jnp
from jax import lax
from jax.experimental import pallas as pl
from jax.experimental.pallas import tpu as pltpu

F32 = jnp.float32
BF16 = jnp.bfloat16

D_MODEL = 1024
GRID_W = 64
EPS = 1e-6
W_BR = 256
CONV_A_K = 31
H_B = 4
HGRN_HD = 64
POOL_WINDOWS = (2, 4, 8, 16)
POOL_GC = 64
D_FF = 2816
N_MIX = 11 * W_BR
N_Z = 9 * W_BR
CTX_SEQ = 256
LAT_SEQ = 2048

INPROJ_TM = 512
MERGE_TM = 512
FFN_TM = 1024
FFN_CHUNK = 256

VMEM_LIMIT = 56 * 1024 * 1024


def _cparams(n_axes=1):
    return pltpu.CompilerParams(dimension_semantics=("arbitrary",) * n_axes, vmem_limit_bytes=VMEM_LIMIT)


def _sigmoid(x):
    return 1.0 / (1.0 + jnp.exp(-x))


def _silu(x):
    return x * _sigmoid(x)


def _rmsnorm(x, g):
    return x * lax.rsqrt(jnp.mean(x * x, axis=-1, keepdims=True) + EPS) * g


def _dot(a, b):
    return jnp.dot(a, b, preferred_element_type=F32)


def _mod_spec(tm, tokens_per_mod, first_row):
    return pl.BlockSpec((1, 1, 6 * D_MODEL), lambda i: (first_row + (i * tm) // tokens_per_mod, 0, 0))


def _mod_kernel(c_ref, w_ref, b_ref, o_ref):
    s = _silu(c_ref[...]).astype(BF16)
    o_ref[0] = _dot(s, w_ref[0].astype(BF16)) + b_ref[0]


def _modulation(c_rows, ada_w, ada_b):
    depth, d, n = ada_w.shape
    nr = c_rows.shape[0]
    tn = 768
    return pl.pallas_call(
        _mod_kernel,
        grid=(depth, n // tn),
        in_specs=[
            pl.BlockSpec((nr, d), lambda l, j: (0, 0)),
            pl.BlockSpec((1, d, tn), lambda l, j: (l, 0, j)),
            pl.BlockSpec((1, 1, tn), lambda l, j: (l, 0, j)),
        ],
        out_specs=pl.BlockSpec((1, nr, tn), lambda l, j: (l, 0, j)),
        out_shape=jax.ShapeDtypeStruct((depth, nr, n), F32),
        compiler_params=_cparams(2),
        name="modulation",
    )(c_rows, ada_w, ada_b.reshape(depth, 1, n))


def _log_forget(z, lb):
    ls = jnp.minimum(z, 0.0) - jnp.log1p(jnp.exp(-jnp.abs(z)))
    a = jnp.log(lb)
    b = jnp.log1p(-lb) + ls
    m = jnp.maximum(a, b)
    return m + jnp.log(jnp.exp(a - m) + jnp.exp(b - m))


def _inproj_kernel(x_ref, mod_ref, g_ref, w_ref, lb_ref, h_ref, z_ref):
    d, w = D_MODEL, W_BR
    mod = mod_ref[0]
    sh, sc = mod[:, 0:d], mod[:, d:2 * d]
    h = (_rmsnorm(x_ref[...], g_ref[...]) * (1.0 + sc) + sh).astype(BF16)
    h_ref[...] = h

    def proj(j):
        return _dot(h, w_ref[:, j * w:(j + 1) * w])

    lb = lb_ref[...]
    z_ref[:, 0 * w:1 * w] = proj(0) * _sigmoid(proj(1))
    z_ref[:, 1 * w:2 * w] = proj(2)
    z_ref[:, 2 * w:3 * w] = _log_forget(proj(3), lb[:, 0:w])
    z_ref[:, 3 * w:4 * w] = _log_forget(proj(4), lb[:, w:2 * w])
    z_ref[:, 4 * w:5 * w] = proj(5)
    z_ref[:, 5 * w:6 * w] = _silu(proj(6))
    z_ref[:, 6 * w:7 * w] = proj(7)
    z_ref[:, 7 * w:8 * w] = proj(8) * proj(9)
    z_ref[:, 8 * w:9 * w] = proj(10)


def _inproj(x, mod, mod_row, g, w_mix, lb, tokens_per_mod):
    t, d = x.shape
    tm = INPROJ_TM
    return pl.pallas_call(
        _inproj_kernel,
        grid=(t // tm,),
        in_specs=[
            pl.BlockSpec((tm, d), lambda i: (i, 0)),
            _mod_spec(tm, tokens_per_mod, mod_row),
            pl.BlockSpec((1, d), lambda i: (0, 0)),
            pl.BlockSpec((d, N_MIX), lambda i: (0, 0)),
            pl.BlockSpec((1, 2 * W_BR), lambda i: (0, 0)),
        ],
        out_specs=[
            pl.BlockSpec((tm, d), lambda i: (i, 0)),
            pl.BlockSpec((tm, N_Z), lambda i: (i, 0)),
        ],
        out_shape=[jax.ShapeDtypeStruct((t, d), BF16), jax.ShapeDtypeStruct((t, N_Z), F32)],
        compiler_params=_cparams(),
        name="inproj",
    )(x, mod, g, w_mix, lb)


def _row_index(rows):
    return lax.broadcasted_iota(jnp.int32, (rows, W_BR), 0)


def _shift_rows(x, j, pos, seg):
    if j == 0:
        return x
    y = pltpu.roll(x, (-j) % x.shape[0], axis=0)
    valid = (pos < seg - j) if j > 0 else (pos >= -j)
    return jnp.where(valid, y, 0.0)


def _conv_a_ln(x, pos, seg, wa_ref, ba_ref, lg_ref, lbeta_ref):
    half = CONV_A_K // 2
    acc = x * wa_ref[half:half + 1, :]
    for j in range(-half, half + 1):
        if j != 0:
            acc = acc + _shift_rows(x, j, pos, seg) * wa_ref[half + j:half + j + 1, :]
    u = acc + ba_ref[...]
    mu = jnp.mean(u, axis=-1, keepdims=True)
    var = jnp.mean(jnp.square(u - mu), axis=-1, keepdims=True)
    return _silu((u - mu) * lax.rsqrt(var + EPS) * lg_ref[...] + lbeta_ref[...])


def _pool_lane_half():
    lane = lax.broadcasted_iota(jnp.int32, (1, W_BR), 1)
    return jnp.left_shift(1, lane // POOL_GC)


def _mix_ctx_kernel(ua_ref, cb_ref, cc_ref, d_ref, wa_ref, ba_ref, lg_ref, lbeta_ref, wc_ref,
                    oa_ref, oc_ref, od_ref):
    seg = CTX_SEQ
    pos = _row_index(seg)
    half = _pool_lane_half()
    max_half = max(POOL_WINDOWS) // 2

    def body(s, carry):
        rows = pl.ds(pl.multiple_of(s * seg, seg), seg)
        oa_ref[rows, :] = _conv_a_ln(ua_ref[rows, :], pos, seg, wa_ref, ba_ref, lg_ref, lbeta_ref).astype(BF16)
        cc = cc_ref[rows, :]
        u = (_shift_rows(cc, -1, pos, seg) * wc_ref[0:1, :] + cc * wc_ref[1:2, :]
             + _shift_rows(cc, 1, pos, seg) * wc_ref[2:3, :])
        oc_ref[rows, :] = (cb_ref[rows, :] * u).astype(BF16)
        dd = d_ref[rows, :]
        acc = jnp.zeros_like(dd)
        cnt = jnp.zeros_like(dd)
        for j in range(-max_half, max_half):
            lane_on = (j >= -half) & (j <= half - 1)
            valid = (pos + j >= 0) & (pos + j < seg) & lane_on
            acc = acc + jnp.where(valid, _shift_rows(dd, j, pos, seg), 0.0)
            cnt = cnt + jnp.where(valid, 1.0, 0.0)
        od_ref[rows, :] = (acc / cnt - dd).astype(BF16)
        return carry

    lax.fori_loop(0, ua_ref.shape[0] // seg, body, 0)


def _mix_lat_kernel(ua_ref, cb_ref, cc_ref, d_ref, wa_ref, ba_ref, lg_ref, lbeta_ref, wc_ref,
                    oa_ref, oc_ref, od_ref):
    seg = GRID_W
    n_rows = LAT_SEQ // GRID_W
    pos = _row_index(seg)
    half = _pool_lane_half()
    max_half = max(POOL_WINDOWS) // 2

    def grid_row(ref, r, j):
        rr = jnp.clip(r + j, 0, n_rows - 1)
        ok = jnp.where(jnp.logical_and(r + j >= 0, r + j < n_rows), 1.0, 0.0)
        return ref[pl.ds(pl.multiple_of(rr * seg, seg), seg), :], ok

    def body(r, carry):
        rows = pl.ds(pl.multiple_of(r * seg, seg), seg)
        oa_ref[rows, :] = _conv_a_ln(ua_ref[rows, :], pos, seg, wa_ref, ba_ref, lg_ref, lbeta_ref).astype(BF16)
        up, ok_up = grid_row(cc_ref, r, -1)
        dn, ok_dn = grid_row(cc_ref, r, 1)
        u = up * (wc_ref[0:1, :] * ok_up) + cc_ref[rows, :] * wc_ref[1:2, :] + dn * (wc_ref[2:3, :] * ok_dn)
        oc_ref[rows, :] = (cb_ref[rows, :] * u).astype(BF16)
        acc = jnp.zeros((seg, W_BR), F32)
        cnt = jnp.zeros((1, W_BR), F32)
        for j in range(-max_half, max_half):
            lane_on = jnp.where((j >= -half) & (j <= half - 1), 1.0, 0.0)
            x, ok = grid_row(d_ref, r, j)
            wgt = lane_on * ok
            acc = acc + x * wgt
            cnt = cnt + wgt
        od_ref[rows, :] = (acc / cnt - d_ref[rows, :]).astype(BF16)
        return carry

    lax.fori_loop(0, n_rows, body, 0)


def _mixers(z, p, latent):
    t = z.shape[0]
    blk = LAT_SEQ
    zspec = lambda col: pl.BlockSpec((blk, W_BR), lambda i, col=col: (i, col))
    full = lambda a: pl.BlockSpec(a.shape, lambda i: (0,) * a.ndim)
    params = [p["conv_a_w"], p["conv_a_b"], p["ln_a_g"], p["ln_a_b"], p["conv_c_w"]]
    out = jax.ShapeDtypeStruct((t, W_BR), BF16)
    return pl.pallas_call(
        _mix_lat_kernel if latent else _mix_ctx_kernel,
        grid=(t // blk,),
        in_specs=[zspec(0), zspec(6), zspec(7), zspec(8)] + [full(a) for a in params],
        out_specs=[pl.BlockSpec((blk, W_BR), lambda i: (i, 0))] * 3,
        out_shape=[out, out, out],
        compiler_params=_cparams(),
        name="mix_lat" if latent else "mix_ctx",
    )(z, z, z, z, *params)


SB = 256
SUB = 8
LEVELS = (8, 16, 32, 64, 128)
NQ = SB // 2


def _level_spans(m, upper):
    off = m if upper else 0
    return [(i * 2 * m + off, i * 2 * m + off + m) for i in range(SB // (2 * m))]


def _take(x, spans):
    parts = [x[a:b] for a, b in spans]
    return parts[0] if len(parts) == 1 else jnp.concatenate(parts, axis=0)


def _level_masks():
    tq = np.arange(H_B * NQ)[:, None] % NQ
    ts = np.arange(NQ)[None, :]
    return np.stack([((tq // m) == (ts // m)) for m in LEVELS[:-1]]).astype(np.float32)


def _cumsum_rows(x):
    g = SB // SUB
    x3 = x.reshape(g, SUB, x.shape[-1])
    sub = lax.broadcasted_iota(jnp.int32, x3.shape, 1)
    step = 1
    while step < SUB:
        x3 = x3 + jnp.where(sub >= step, pltpu.roll(x3, step, axis=1), 0.0)
        step *= 2
    tot = x3[:, SUB - 1:SUB, :]
    offs = [jnp.zeros_like(tot[0:1])]
    for i in range(1, g):
        offs.append(offs[-1] + tot[i - 1:i])
    return (x3 + jnp.concatenate(offs, axis=0)).reshape(x.shape)


def _hgrn_kernel(*refs, has_s0):
    if has_s0:
        q_ref, lff_ref, lfb_ref, v_ref, sg_ref, ng_ref, hm_ref, lm_ref, s0_ref, o_ref, oacc, st = refs
        sfin_ref = None
    else:
        q_ref, lff_ref, lfb_ref, v_ref, sg_ref, ng_ref, hm_ref, lm_ref, o_ref, sfin_ref, oacc, st = refs
        s0_ref = None
    seq, w = q_ref.shape
    n_sb = seq // SB
    with_inter = has_s0 or n_sb > 1
    head_ones = hm_ref[...].astype(BF16)
    lane_head = lax.broadcasted_iota(jnp.int32, (1, w), 1) // HGRN_HD
    in_head = [lane_head == h for h in range(H_B)]
    sub3 = lax.broadcasted_iota(jnp.int32, (SB // SUB, SUB, w), 1)
    nt = (((1,), (1,)), ((), ()))
    tn = (((0,), (0,)), ((), ()))
    head_blocks = [slice(h * HGRN_HD, (h + 1) * HGRN_HD) for h in range(H_B)]

    for direction in range(2):
        st[direction] = jnp.zeros((w, w), F32)
        if s0_ref is not None:
            for h, blk in enumerate(head_blocks):
                st[direction, blk, blk] = s0_ref[0, direction, h]
            st[direction] = st[direction].T

    def sb_body(si, carry):
        for direction, lf_ref in ((0, lff_ref), (1, lfb_ref)):
            back = direction == 1
            base = pl.multiple_of(((n_sb - 1 - si) if back else si) * SB, SB)
            rows = pl.ds(base, SB)
            lf = lf_ref[rows, :]
            q = q_ref[rows, :]
            v = v_ref[rows, :]
            vb = v.astype(BF16)
            key = 1.0 - jnp.exp(lf)
            g = _cumsum_rows(lf)
            tot = g[SB - 1:SB, :]
            b = (lf - g) if back else g

            q3, k3, b3, v3 = (x.reshape(SB // SUB, SUB, w) for x in (q, key, b, v))
            acc = _dot((q * key).astype(BF16), head_ones) * v
            for dist in range(1, SUB):
                sft = (SUB - dist) if back else dist
                valid = (sub3 < SUB - dist) if back else (sub3 >= dist)
                e = jnp.where(valid, q3 * pltpu.roll(k3, sft, axis=1) * jnp.exp(b3 - pltpu.roll(b3, sft, axis=1)), 0.0)
                red = _dot(e.reshape(SB, w).astype(BF16), head_ones)
                acc = acc + red * pltpu.roll(v3, sft, axis=1).reshape(SB, w)
            oacc[direction, rows, :] = acc

            for li, m in enumerate(LEVELS):
                lower, upper = _level_spans(m, False), _level_spans(m, True)
                qs, ks = (lower, upper) if back else (upper, lower)
                edge = [(a + m) if back else (a + m - 1) for a, _ in lower]
                bm = jnp.concatenate([jnp.broadcast_to(b[r:r + 1], (m, w)) for r in edge], axis=0)
                qg = _take(q, qs) * jnp.exp(_take(b, qs) - bm)
                kg = (_take(key, ks) * jnp.exp(bm - _take(b, ks))).astype(BF16)
                stacked = jnp.concatenate([jnp.where(in_head[h], qg, 0.0) for h in range(H_B)], axis=0)
                p = lax.dot_general(stacked.astype(BF16), kg, nt, preferred_element_type=F32)
                if m < NQ:
                    p = p * lm_ref[li]
                r = _dot(p.astype(BF16), _take(vb, ks))
                res = jnp.where(in_head[0], r[0:NQ], 0.0)
                for h in range(1, H_B):
                    res = res + jnp.where(in_head[h], r[h * NQ:(h + 1) * NQ], 0.0)
                for bi, (a0, _) in enumerate(qs):
                    dst = pl.ds(base + a0, m)
                    oacc[direction, dst, :] = oacc[direction, dst, :] + res[bi * m:(bi + 1) * m]

            cq, ck = (tot, 0.0) if back else (0.0, tot)
            s_old = st[direction]
            if with_inter:
                qp = (q * jnp.exp(b + cq)).astype(BF16)
                oacc[direction, rows, :] = oacc[direction, rows, :] + lax.dot_general(
                    qp, s_old.astype(BF16), nt, preferred_element_type=F32)
            kp = (key * jnp.exp(ck - b)).astype(BF16)
            kv_t = lax.dot_general(vb, kp, tn, preferred_element_type=F32)
            st[direction] = s_old * jnp.exp(tot) + kv_t * hm_ref[...]
        return carry

    lax.fori_loop(0, n_sb, sb_body, 0)
    if sfin_ref is not None:
        for direction in range(2):
            s_kv = st[direction].T
            for h, blk in enumerate(head_blocks):
                sfin_ref[0, direction, h] = s_kv[blk, blk]

    def out_body(i, carry):
        rows = pl.ds(pl.multiple_of(i * SB, SB), SB)
        o = oacc[0, rows, :] + oacc[1, rows, :]
        o2 = o * o
        hi = o2.astype(BF16)
        lo = (o2 - hi.astype(F32)).astype(BF16)
        ms = (_dot(hi, head_ones) + _dot(lo, head_ones)) * (1.0 / HGRN_HD)
        o_ref[rows, :] = (o * lax.rsqrt(ms + EPS) * ng_ref[...] * sg_ref[rows, :]).astype(BF16)
        return carry

    lax.fori_loop(0, n_sb, out_body, 0)


def _hgrn(z, norm_g, head_mask, level_masks, s0, seq):
    t = z.shape[0]
    n_seq = t // seq
    zspec = lambda col: pl.BlockSpec((seq, W_BR), lambda i, col=col: (i, col))
    wb = H_B * HGRN_HD
    state_spec = pl.BlockSpec((1, 2, H_B, HGRN_HD, HGRN_HD), lambda i: (i, 0, 0, 0, 0))
    in_specs = [zspec(1), zspec(2), zspec(3), zspec(4), zspec(5),
                pl.BlockSpec((1, wb), lambda i: (0, 0)),
                pl.BlockSpec((wb, wb), lambda i: (0, 0)),
                pl.BlockSpec(level_masks.shape, lambda i: (0, 0, 0))]
    args = [z, z, z, z, z, norm_g, head_mask, level_masks]
    out_specs = [pl.BlockSpec((seq, wb), lambda i: (i, 0))]
    out_shape = [jax.ShapeDtypeStruct((t, wb), BF16)]
    if s0 is not None:
        in_specs.append(state_spec)
        args.append(s0)
    else:
        out_specs.append(state_spec)
        out_shape.append(jax.ShapeDtypeStruct((n_seq, 2, H_B, HGRN_HD, HGRN_HD), F32))
    return pl.pallas_call(
        functools.partial(_hgrn_kernel, has_s0=s0 is not None),
        grid=(n_seq,),
        in_specs=in_specs,
        out_specs=out_specs,
        out_shape=out_shape,
        scratch_shapes=[pltpu.VMEM((2, seq, wb), F32), pltpu.VMEM((2, wb, wb), F32)],
        compiler_params=_cparams(),
        name="hgrn_lat" if s0 is not None else "hgrn_ctx",
    )(*args)


def _merge_kernel(h_ref, ua_ref, ub_ref, uc_ref, ud_ref, x_ref, mod_ref, wg_ref, wa_ref, wb_ref, wc_ref, wd_ref,
                  ps_ref, wo_ref, g2_ref, x1_ref, h2_ref):
    d = D_MODEL
    h = h_ref[...]
    merged = None
    for i, (u_ref, w_ref) in enumerate(((ua_ref, wa_ref), (ub_ref, wb_ref), (uc_ref, wc_ref), (ud_ref, wd_ref))):
        y = _dot(u_ref[...], w_ref[...])
        if i == 3:
            y = y * ps_ref[...]
        term = _sigmoid(_dot(h, wg_ref[:, i * d:(i + 1) * d])) * y
        merged = term if merged is None else merged + term
    mix = _dot(merged.astype(BF16), wo_ref[...])
    mod = mod_ref[0]
    g_m, sh_f, sc_f = mod[:, 2 * d:3 * d], mod[:, 3 * d:4 * d], mod[:, 4 * d:5 * d]
    x1 = x_ref[...] + g_m * mix
    x1_ref[...] = x1
    h2_ref[...] = (_rmsnorm(x1, g2_ref[...]) * (1.0 + sc_f) + sh_f).astype(BF16)


def _merge(h, us, x, mod, mod_row, w, tokens_per_mod):
    t, d = x.shape
    tm = MERGE_TM
    tok = lambda width: pl.BlockSpec((tm, width), lambda i: (i, 0))
    full = lambda a: pl.BlockSpec(a.shape, lambda i: (0,) * a.ndim)
    weights = [w["w_gate"], w["w_out_a"], w["w_out_b"], w["w_out_c"], w["w_pool"], w["pool_scale"], w["w_o"],
               w["norm_ffn_g"]]
    return pl.pallas_call(
        _merge_kernel,
        grid=(t // tm,),
        in_specs=[tok(d)] + [tok(W_BR)] * 4 + [tok(d), _mod_spec(tm, tokens_per_mod, mod_row)]
                 + [full(a) for a in weights],
        out_specs=[tok(d), tok(d)],
        out_shape=[jax.ShapeDtypeStruct((t, d), F32), jax.ShapeDtypeStruct((t, d), BF16)],
        compiler_params=_cparams(),
        name="merge",
    )(h, *us, x, mod, *weights)


def _ffn_kernel(h2_ref, x1_ref, mod_ref, w13_ref, w2_ref, gf_ref, o_ref, *, final):
    d = D_MODEL
    h2 = h2_ref[...]
    acc = None
    for lo in range(0, D_FF, FFN_CHUNK):
        hi = min(lo + FFN_CHUNK, D_FF)
        gate = _dot(h2, w13_ref[:, lo:hi])
        up = _dot(h2, w13_ref[:, D_FF + lo:D_FF + hi])
        part = _dot((_silu(gate) * up).astype(BF16), w2_ref[lo:hi, :])
        acc = part if acc is None else acc + part
    g_f = mod_ref[0][:, 5 * d:6 * d]
    x2 = x1_ref[...] + g_f * acc
    if final:
        x2 = _rmsnorm(x2, gf_ref[...])
    o_ref[...] = x2


def _ffn(h2, x1, mod, mod_row, w13, w2, final_g, tokens_per_mod, final):
    t, d = x1.shape
    tm = FFN_TM
    full = lambda a: pl.BlockSpec(a.shape, lambda i: (0,) * a.ndim)
    return pl.pallas_call(
        functools.partial(_ffn_kernel, final=final),
        grid=(t // tm,),
        in_specs=[pl.BlockSpec((tm, d), lambda i: (i, 0)), pl.BlockSpec((tm, d), lambda i: (i, 0)),
                  _mod_spec(tm, tokens_per_mod, mod_row), full(w13), full(w2), full(final_g)],
        out_specs=pl.BlockSpec((tm, d), lambda i: (i, 0)),
        out_shape=jax.ShapeDtypeStruct((t, d), F32),
        compiler_params=_cparams(),
        name="ffn",
    )(h2, x1, mod, w13, w2, final_g)


def _block_diag_heads(s):
    eye = jnp.eye(H_B, dtype=s.dtype)
    bd = jnp.einsum("...hab,hg->...hagb", s, eye)
    return bd.reshape(s.shape[:-3] + (H_B * s.shape[-2], H_B * s.shape[-1]))


def kernel(x_prompt, x_sample, state_hgrn, c, c_ctx, ada_w, ada_b, norm_mix_g, w_in, conv_a_w, conv_a_b, ln_a_g,
           ln_a_b, w_out_a, hgrn_lb_logits, hgrn_norm_g, w_out_b, conv_c_w, w_out_c, pool_w, pool_scale, w_o,
           norm_ffn_g, ffn_w13, ffn_w2, final_norm_g):
    depth = w_in.shape[0]
    d = D_MODEL
    n_ctx, n_lat = x_prompt.shape[0], x_sample.shape[0]
    t_ctx, t_lat = n_ctx * CTX_SEQ, n_lat * LAT_SEQ

    n_rows = -(-(1 + n_lat) // 8) * 8
    c_rows = jnp.zeros((n_rows, d), F32).at[0].set(c_ctx).at[1:1 + n_lat].set(c)
    mod = _modulation(c_rows, ada_w, ada_b).reshape(depth * n_rows, 1, 6 * d)

    lb_all = jnp.cumsum(jax.nn.softmax(hgrn_lb_logits.astype(F32), axis=0), axis=0)
    lb_all = (lb_all - lb_all[:1]).reshape(depth, 1, 2 * W_BR)

    head_mask = _block_diag_heads(jnp.ones((H_B, HGRN_HD, HGRN_HD), F32))
    level_masks = jnp.asarray(_level_masks())

    row = lambda a: a.reshape(1, -1)
    y_p = x_prompt.reshape(t_ctx, d)
    y_s = x_sample.reshape(t_lat, d)
    ctx_states = []
    for l in range(depth):
        w = {
            "w_mix": w_in[l, :, :N_MIX].astype(BF16), "w_gate": w_in[l, :, N_MIX:].astype(BF16),
            "w_out_a": w_out_a[l].astype(BF16), "w_out_b": w_out_b[l].astype(BF16),
            "w_out_c": w_out_c[l].astype(BF16),
            "w_pool": _block_diag_heads(pool_w[l]).astype(BF16),
            "pool_scale": row(pool_scale[l]), "w_o": w_o[l].astype(BF16), "norm_ffn_g": row(norm_ffn_g[l]),
            "conv_a_w": conv_a_w[l], "conv_a_b": row(conv_a_b[l]), "ln_a_g": row(ln_a_g[l]),
            "ln_a_b": row(ln_a_b[l]), "conv_c_w": conv_c_w[l],
        }
        w13 = ffn_w13[l].astype(BF16)
        w2 = ffn_w2[l].astype(BF16)
        final = l == depth - 1
        outs = []
        for latent, x, tokens_per_mod in ((False, y_p, t_ctx), (True, y_s, LAT_SEQ)):
            mod_row = l * n_rows + (1 if latent else 0)
            h, z = _inproj(x, mod, mod_row, row(norm_mix_g[l]), w["w_mix"], lb_all[l], tokens_per_mod)
            u_a, u_c, u_d = _mixers(z, w, latent)
            hg = _hgrn(z, row(hgrn_norm_g[l]), head_mask, level_masks, state_hgrn[:, l] if latent else None,
                       LAT_SEQ if latent else CTX_SEQ)
            if not latent:
                ctx_states.append(hg[1])
            x1, h2 = _merge(h, (u_a, hg[0], u_c, u_d), x, mod, mod_row, w, tokens_per_mod)
            outs.append(_ffn(h2, x1, mod, mod_row, w13, w2, row(final_norm_g), tokens_per_mod, final))
        y_p, y_s = outs

    new_state = jnp.stack(ctx_states, axis=1).astype(x_prompt.dtype)
    return (y_p.reshape(x_prompt.shape), y_s.reshape(x_sample.shape), new_state)
```

```python
import functools

import numpy as np
import jax
import jax.numpy as jnp
from jax import lax
from jax.experimental import pallas as pl
from jax.experimental.pallas import tpu as pltpu

F32 = jnp.float32
BF16 = jnp.bfloat16

D_MODEL = 1024
GRID_W = 64
EPS = 1e-6
W_BR = 256
CONV_A_K = 31
H_B = 4
HGRN_HD = 64
POOL_WINDOWS = (2, 4, 8, 16)
POOL_GC = 64
D_FF = 2816
N_MIX = 11 * W_BR
N_Z = 9 * W_BR
CTX_SEQ = 256
LAT_SEQ = 2048

INPROJ_TM = 512
MERGE_TM = 256
FFN_TM = 1024
FFN_CHUNK = 256

VMEM_LIMIT = 56 * 1024 * 1024
LOG2E = 1.4426950408889634


def _cparams(n_axes=1):
    return pltpu.CompilerParams(dimension_semantics=("arbitrary",) * n_axes, vmem_limit_bytes=VMEM_LIMIT)


def _sigmoid(x):
    return 1.0 / (1.0 + jnp.exp(-x))


def _silu(x):
    return x * _sigmoid(x)


def _rmsnorm(x, g):
    return x * lax.rsqrt(jnp.mean(x * x, axis=-1, keepdims=True) + EPS) * g


def _dot(a, b):
    return jnp.dot(a, b, preferred_element_type=F32)


def _layer_spec(a, layer):
    return pl.BlockSpec((1,) + a.shape[1:], lambda i: (layer,) + (0,) * (a.ndim - 1))


def _mod_spec(tm, tokens_per_mod, first_row):
    return pl.BlockSpec((1, 1, 6 * D_MODEL), lambda i: (first_row + (i * tm) // tokens_per_mod, 0, 0))


def _mod_kernel(c_ref, w_ref, b_ref, o_ref):
    s = _silu(c_ref[...]).astype(BF16)
    o_ref[0] = _dot(s, w_ref[0].astype(BF16)) + b_ref[0]


def _modulation(c_rows, ada_w, ada_b):
    depth, d, n = ada_w.shape
    nr = c_rows.shape[0]
    tn = 768
    return pl.pallas_call(
        _mod_kernel,
        grid=(depth, n // tn),
        in_specs=[
            pl.BlockSpec((nr, d), lambda l, j: (0, 0)),
            pl.BlockSpec((1, d, tn), lambda l, j: (l, 0, j)),
            pl.BlockSpec((1, 1, tn), lambda l, j: (l, 0, j)),
        ],
        out_specs=pl.BlockSpec((1, nr, tn), lambda l, j: (l, 0, j)),
        out_shape=jax.ShapeDtypeStruct((depth, nr, n), F32),
        compiler_params=_cparams(2),
        name="modulation",
    )(c_rows, ada_w, ada_b.reshape(depth, 1, n))


def _log_forget(z, lb):
    ls = jnp.minimum(z, 0.0) - jnp.log1p(jnp.exp(-jnp.abs(z)))
    a = jnp.log(lb)
    b = jnp.log1p(-lb) + ls
    return jnp.maximum(a, b) + jnp.log1p(jnp.exp(-jnp.abs(a - b)))


def _inproj_kernel(x_ref, mod_ref, g_ref, w_ref, lb_ref, h_ref, z_ref):
    d, w = D_MODEL, W_BR
    mod = mod_ref[0]
    sh, sc = mod[:, 0:d], mod[:, d:2 * d]
    h = (_rmsnorm(x_ref[...], g_ref[0]) * (1.0 + sc) + sh).astype(BF16)
    h_ref[...] = h

    def proj(j, n):
        return _dot(h, w_ref[:, j * w:(j + n) * w])

    lb = lb_ref[0]
    za = proj(0, 2)
    z_ref[:, 0 * w:1 * w] = za[:, 0:w] * _sigmoid(za[:, w:2 * w])
    zb = proj(2, 3)
    z_ref[:, 1 * w:2 * w] = zb[:, 0:w]
    z_ref[:, 2 * w:3 * w] = _log_forget(zb[:, w:2 * w], lb[:, 0:w]) * LOG2E
    z_ref[:, 3 * w:4 * w] = _log_forget(zb[:, 2 * w:3 * w], lb[:, w:2 * w]) * LOG2E
    zv = proj(5, 2)
    z_ref[:, 4 * w:5 * w] = zv[:, 0:w]
    z_ref[:, 5 * w:6 * w] = _silu(zv[:, w:2 * w])
    zc = proj(7, 3)
    z_ref[:, 6 * w:7 * w] = zc[:, 0:w]
    z_ref[:, 7 * w:8 * w] = zc[:, w:2 * w] * zc[:, 2 * w:3 * w]
    z_ref[:, 8 * w:9 * w] = proj(10, 1)


def _inproj(x, mod, mod_row, g, w_mix, lb, layer, tokens_per_mod):
    t, d = x.shape
    tm = INPROJ_TM
    return pl.pallas_call(
        _inproj_kernel,
        grid=(t // tm,),
        in_specs=[
            pl.BlockSpec((tm, d), lambda i: (i, 0)),
            _mod_spec(tm, tokens_per_mod, mod_row),
            _layer_spec(g, layer),
            pl.BlockSpec((d, N_MIX), lambda i: (0, 0)),
            _layer_spec(lb, layer),
        ],
        out_specs=[
            pl.BlockSpec((tm, d), lambda i: (i, 0)),
            pl.BlockSpec((tm, N_Z), lambda i: (i, 0)),
        ],
        out_shape=[jax.ShapeDtypeStruct((t, d), BF16), jax.ShapeDtypeStruct((t, N_Z), F32)],
        compiler_params=_cparams(),
        name="inproj",
    )(x, mod, g, w_mix, lb)


def _row_index(rows):
    return lax.broadcasted_iota(jnp.int32, (rows, W_BR), 0)


def _shift_rows(x, j, pos, seg):
    if j == 0:
        return x
    y = pltpu.roll(x, (-j) % x.shape[0], axis=0)
    valid = (pos < seg - j) if j > 0 else (pos >= -j)
    return jnp.where(valid, y, 0.0)


def _conv_a_ln(x, pos, seg, wa_ref, ba_ref, lg_ref, lbeta_ref):
    half = CONV_A_K // 2
    acc = x * wa_ref[half:half + 1, :]
    for j in range(-half, half + 1):
        if j != 0:
            acc = acc + _shift_rows(x, j, pos, seg) * wa_ref[half + j:half + j + 1, :]
    u = acc + ba_ref[...]
    mu = jnp.mean(u, axis=-1, keepdims=True)
    var = jnp.mean(jnp.square(u - mu), axis=-1, keepdims=True)
    return _silu((u - mu) * lax.rsqrt(var + EPS) * lg_ref[...] + lbeta_ref[...])


LANE_TILE = 128


def _pool_lane_tiles():
    per_tile = LANE_TILE // POOL_GC
    tiles = []
    for i in range(W_BR // LANE_TILE):
        lane = lax.broadcasted_iota(jnp.int32, (1, LANE_TILE), 1) + i * LANE_TILE
        half = jnp.left_shift(1, lane // POOL_GC)
        reach = max(POOL_WINDOWS[i * per_tile:(i + 1) * per_tile]) // 2
        tiles.append((slice(i * LANE_TILE, (i + 1) * LANE_TILE), half, reach))
    return tiles


def _mix_ctx_kernel(ua_ref, cb_ref, cc_ref, d_ref, wa_ref, ba_ref, lg_ref, lbeta_ref, wc_ref,
                    oa_ref, oc_ref, od_ref):
    wa_ref, ba_ref, lg_ref, lbeta_ref, wc_ref = (r.at[0] for r in (wa_ref, ba_ref, lg_ref, lbeta_ref, wc_ref))
    seg = CTX_SEQ
    pos = _row_index(seg)
    posl = lax.broadcasted_iota(jnp.int32, (seg, LANE_TILE), 0)
    tiles = _pool_lane_tiles()

    def body(s, carry):
        rows = pl.ds(pl.multiple_of(s * seg, seg), seg)
        oa_ref[rows, :] = _conv_a_ln(ua_ref[rows, :], pos, seg, wa_ref, ba_ref, lg_ref, lbeta_ref).astype(BF16)
        cc = cc_ref[rows, :]
        u = (_shift_rows(cc, -1, pos, seg) * wc_ref[0:1, :] + cc * wc_ref[1:2, :]
             + _shift_rows(cc, 1, pos, seg) * wc_ref[2:3, :])
        oc_ref[rows, :] = (cb_ref[rows, :] * u).astype(BF16)
        for lanes, half, reach in tiles:
            dd = d_ref[rows, lanes]
            acc = jnp.zeros_like(dd)
            cnt = jnp.zeros_like(dd)
            for j in range(-reach, reach):
                lane_on = (j >= -half) & (j <= half - 1)
                valid = (posl + j >= 0) & (posl + j < seg) & lane_on
                acc = acc + jnp.where(valid, _shift_rows(dd, j, posl, seg), 0.0)
                cnt = cnt + jnp.where(valid, 1.0, 0.0)
            od_ref[rows, lanes] = (acc / cnt - dd).astype(BF16)
        return carry

    lax.fori_loop(0, ua_ref.shape[0] // seg, body, 0)


def _mix_lat_kernel(ua_ref, cb_ref, cc_ref, d_ref, wa_ref, ba_ref, lg_ref, lbeta_ref, wc_ref,
                    oa_ref, oc_ref, od_ref):
    wa_ref, ba_ref, lg_ref, lbeta_ref, wc_ref = (r.at[0] for r in (wa_ref, ba_ref, lg_ref, lbeta_ref, wc_ref))
    seg = GRID_W
    n_rows = LAT_SEQ // GRID_W
    pos = _row_index(seg)
    tiles = _pool_lane_tiles()

    def grid_row(ref, r, j, lanes=slice(None)):
        rr = jnp.clip(r + j, 0, n_rows - 1)
        ok = jnp.where(jnp.logical_and(r + j >= 0, r + j < n_rows), 1.0, 0.0)
        return ref[pl.ds(pl.multiple_of(rr * seg, seg), seg), lanes], ok

    def body(r, carry):
        rows = pl.ds(pl.multiple_of(r * seg, seg), seg)
        oa_ref[rows, :] = _conv_a_ln(ua_ref[rows, :], pos, seg, wa_ref, ba_ref, lg_ref, lbeta_ref).astype(BF16)
        up, ok_up = grid_row(cc_ref, r, -1)
        dn, ok_dn = grid_row(cc_ref, r, 1)
        u = up * (wc_ref[0:1, :] * ok_up) + cc_ref[rows, :] * wc_ref[1:2, :] + dn * (wc_ref[2:3, :] * ok_dn)
        oc_ref[rows, :] = (cb_ref[rows, :] * u).astype(BF16)
        for lanes, half, reach in tiles:
            acc = jnp.zeros((seg, LANE_TILE), F32)
            cnt = jnp.zeros((1, LANE_TILE), F32)
            for j in range(-reach, reach):
                lane_on = jnp.where((j >= -half) & (j <= half - 1), 1.0, 0.0)
                x, ok = grid_row(d_ref, r, j, lanes)
                wgt = lane_on * ok
                acc = acc + x * wgt
                cnt = cnt + wgt
            od_ref[rows, lanes] = (acc / cnt - d_ref[rows, lanes]).astype(BF16)
        return carry

    lax.fori_loop(0, n_rows, body, 0)


def _mixers(z, p, layer, latent):
    t = z.shape[0]
    blk = LAT_SEQ
    zspec = lambda col: pl.BlockSpec((blk, W_BR), lambda i, col=col: (i, col))
    params = [p["conv_a_w"], p["conv_a_b"], p["ln_a_g"], p["ln_a_b"], p["conv_c_w"]]
    out = jax.ShapeDtypeStruct((t, W_BR), BF16)
    return pl.pallas_call(
        _mix_lat_kernel if latent else _mix_ctx_kernel,
        grid=(t // blk,),
        in_specs=[zspec(0), zspec(6), zspec(7), zspec(8)] + [_layer_spec(a, layer) for a in params],
        out_specs=[pl.BlockSpec((blk, W_BR), lambda i: (i, 0))] * 3,
        out_shape=[out, out, out],
        compiler_params=_cparams(),
        name="mix_lat" if latent else "mix_ctx",
    )(z, z, z, z, *params)


SB = 256
SUB = 8
LEVELS = (8, 16, 32, 64, 128)
NQ = SB // 2


def _level_spans(m, upper):
    off = m if upper else 0
    return [(i * 2 * m + off, i * 2 * m + off + m) for i in range(SB // (2 * m))]


def _take(x, spans):
    parts = [x[a:b] for a, b in spans]
    return parts[0] if len(parts) == 1 else jnp.concatenate(parts, axis=0)


def _level_masks():
    tq = np.arange(H_B * NQ)[:, None] % NQ
    ts = np.arange(NQ)[None, :]
    return np.stack([((tq // m) == (ts // m)) for m in LEVELS[:-1]]).astype(np.float32)


def _cumsum_rows(x):
    g = SB // SUB
    x3 = x.reshape(g, SUB, x.shape[-1])
    sub = lax.broadcasted_iota(jnp.int32, x3.shape, 1)
    step = 1
    while step < SUB:
        x3 = x3 + jnp.where(sub >= step, pltpu.roll(x3, step, axis=1), 0.0)
        step *= 2
    tot = x3[:, SUB - 1:SUB, :]
    offs = [jnp.zeros_like(tot[0:1])]
    for i in range(1, g):
        offs.append(offs[-1] + tot[i - 1:i])
    return (x3 + jnp.concatenate(offs, axis=0)).reshape(x.shape)


def _hgrn_kernel(*refs, has_s0):
    if has_s0:
        q_ref, lff_ref, lfb_ref, v_ref, sg_ref, ng_ref, hm_ref, lm_ref, s0_ref, o_ref, oacc, st = refs
        sfin_ref = None
    else:
        q_ref, lff_ref, lfb_ref, v_ref, sg_ref, ng_ref, hm_ref, lm_ref, o_ref, sfin_ref, oacc, st = refs
        s0_ref = None
    seq, w = q_ref.shape
    n_sb = seq // SB
    with_inter = has_s0 or n_sb > 1
    head_ones = hm_ref[...].astype(BF16)
    lane_head = lax.broadcasted_iota(jnp.int32, (1, w), 1) // HGRN_HD
    in_head = [lane_head == h for h in range(H_B)]
    sub3 = lax.broadcasted_iota(jnp.int32, (SB // SUB, SUB, w), 1)
    nt = (((1,), (1,)), ((), ()))
    tn = (((0,), (0,)), ((), ()))
    head_blocks = [slice(h * HGRN_HD, (h + 1) * HGRN_HD) for h in range(H_B)]

    for direction in range(2 if with_inter else 0):
        st[direction] = jnp.zeros((w, w), F32)
        if s0_ref is not None:
            for h, blk in enumerate(head_blocks):
                st[direction, blk, blk] = s0_ref[0, 0, direction, h]
            st[direction] = st[direction].T

    def sb_body(si, carry):
        for direction, lf_ref in ((0, lff_ref), (1, lfb_ref)):
            back = direction == 1
            base = pl.multiple_of(((n_sb - 1 - si) if back else si) * SB, SB)
            rows = pl.ds(base, SB)
            lf = lf_ref[rows, :]
            q = q_ref[rows, :]
            v = v_ref[rows, :]
            vb = v.astype(BF16)
            key = 1.0 - jnp.exp2(lf)
            g = _cumsum_rows(lf)
            tot = g[SB - 1:SB, :]
            b = (lf - g) if back else g

            q3, k3, b3, v3 = (x.reshape(SB // SUB, SUB, w) for x in (q, key, b, v))
            acc = _dot((q * key).astype(BF16), head_ones) * v
            for dist in range(1, SUB):
                sft = (SUB - dist) if back else dist
                valid = (sub3 < SUB - dist) if back else (sub3 >= dist)
                e = jnp.where(valid, q3 * pltpu.roll(k3, sft, axis=1) * jnp.exp2(b3 - pltpu.roll(b3, sft, axis=1)), 0.0)
                red = _dot(e.reshape(SB, w).astype(BF16), head_ones)
                acc = acc + red * pltpu.roll(v3, sft, axis=1).reshape(SB, w)
            oacc[direction, rows, :] = acc

            for li, m in enumerate(LEVELS):
                lower, upper = _level_spans(m, False), _level_spans(m, True)
                qs, ks = (lower, upper) if back else (upper, lower)
                edge = [(a + m) if back else (a + m - 1) for a, _ in lower]
                bm = jnp.concatenate([jnp.broadcast_to(b[r:r + 1], (m, w)) for r in edge], axis=0)
                qg = _take(q, qs) * jnp.exp2(_take(b, qs) - bm)
                kg = (_take(key, ks) * jnp.exp2(bm - _take(b, ks))).astype(BF16)
                stacked = jnp.concatenate([jnp.where(in_head[h], qg, 0.0) for h in range(H_B)], axis=0)
                p = lax.dot_general(stacked.astype(BF16), kg, nt, preferred_element_type=F32)
                if m < NQ:
                    p = p * lm_ref[li]
                r = _dot(p.astype(BF16), _take(vb, ks))
                res = jnp.where(in_head[0], r[0:NQ], 0.0)
                for h in range(1, H_B):
                    res = res + jnp.where(in_head[h], r[h * NQ:(h + 1) * NQ], 0.0)
                for bi, (a0, _) in enumerate(qs):
                    dst = pl.ds(base + a0, m)
                    oacc[direction, dst, :] = oacc[direction, dst, :] + res[bi * m:(bi + 1) * m]

            cq, ck = (tot, 0.0) if back else (0.0, tot)
            kp = (key * jnp.exp2(ck - b)).astype(BF16)
            if with_inter:
                s_old = st[direction]
                qp = (q * jnp.exp2(b + cq)).astype(BF16)
                oacc[direction, rows, :] = oacc[direction, rows, :] + lax.dot_general(
                    qp, s_old.astype(BF16), nt, preferred_element_type=F32)
                kv_t = lax.dot_general(vb, kp, tn, preferred_element_type=F32)
                st[direction] = s_old * jnp.exp2(tot) + kv_t * hm_ref[...]
            else:
                kv = lax.dot_general(kp, vb, tn, preferred_element_type=F32)
                for h, blk in enumerate(head_blocks):
                    sfin_ref[0, direction, h] = kv[blk, blk]
        return carry

    lax.fori_loop(0, n_sb, sb_body, 0)
    if sfin_ref is not None and with_inter:
        for direction in range(2):
            s_kv = st[direction].T
            for h, blk in enumerate(head_blocks):
                sfin_ref[0, direction, h] = s_kv[blk, blk]

    def out_body(i, carry):
        rows = pl.ds(pl.multiple_of(i * SB, SB), SB)
        o = oacc[0, rows, :] + oacc[1, rows, :]
        o2 = o * o
        hi = o2.astype(BF16)
        lo = (o2 - hi.astype(F32)).astype(BF16)
        ms = (_dot(hi, head_ones) + _dot(lo, head_ones)) * (1.0 / HGRN_HD)
        o_ref[rows, :] = (o * lax.rsqrt(ms + EPS) * ng_ref[0] * sg_ref[rows, :]).astype(BF16)
        return carry

    lax.fori_loop(0, n_sb, out_body, 0)


def _hgrn(z, norm_g, head_mask, level_masks, s0, layer, seq):
    t = z.shape[0]
    n_seq = t // seq
    zspec = lambda col: pl.BlockSpec((seq, W_BR), lambda i, col=col: (i, col))
    wb = H_B * HGRN_HD
    state_block = (1, 2, H_B, HGRN_HD, HGRN_HD)
    in_specs = [zspec(1), zspec(2), zspec(3), zspec(4), zspec(5),
                _layer_spec(norm_g, layer),
                pl.BlockSpec((wb, wb), lambda i: (0, 0)),
                pl.BlockSpec(level_masks.shape, lambda i: (0, 0, 0))]
    args = [z, z, z, z, z, norm_g, head_mask, level_masks]
    out_specs = [pl.BlockSpec((seq, wb), lambda i: (i, 0))]
    out_shape = [jax.ShapeDtypeStruct((t, wb), BF16)]
    if s0 is not None:
        in_specs.append(pl.BlockSpec((1,) + state_block, lambda i: (i, layer, 0, 0, 0, 0)))
        args.append(s0)
    else:
        out_specs.append(pl.BlockSpec(state_block, lambda i: (i, 0, 0, 0, 0)))
        out_shape.append(jax.ShapeDtypeStruct((n_seq, 2, H_B, HGRN_HD, HGRN_HD), F32))
    return pl.pallas_call(
        functools.partial(_hgrn_kernel, has_s0=s0 is not None),
        grid=(n_seq,),
        in_specs=in_specs,
        out_specs=out_specs,
        out_shape=out_shape,
        scratch_shapes=[pltpu.VMEM((2, seq, wb), F32), pltpu.VMEM((2, wb, wb), F32)],
        compiler_params=_cparams(),
        name="hgrn_lat" if s0 is not None else "hgrn_ctx",
    )(*args)


def _merge_kernel(h_ref, ua_ref, ub_ref, uc_ref, ud_ref, x_ref, mod_ref, wg_ref, wa_ref, wb_ref, wc_ref, wd_ref,
                  ps_ref, wo_ref, g2_ref, x1_ref, h2_ref):
    d = D_MODEL
    h = h_ref[...]
    merged = None
    for i, (u_ref, w_ref) in enumerate(((ua_ref, wa_ref), (ub_ref, wb_ref), (uc_ref, wc_ref), (ud_ref, wd_ref))):
        y = _dot(u_ref[...], w_ref[0])
        if i == 3:
            y = y * ps_ref[0]
        term = _sigmoid(_dot(h, wg_ref[:, i * d:(i + 1) * d])) * y
        merged = term if merged is None else merged + term
    mix = _dot(merged.astype(BF16), wo_ref[0])
    mod = mod_ref[0]
    g_m, sh_f, sc_f = mod[:, 2 * d:3 * d], mod[:, 3 * d:4 * d], mod[:, 4 * d:5 * d]
    x1 = x_ref[...] + g_m * mix
    x1_ref[...] = x1
    h2_ref[...] = (_rmsnorm(x1, g2_ref[0]) * (1.0 + sc_f) + sh_f).astype(BF16)


def _merge(h, us, x, mod, mod_row, w_gate, w, layer, tokens_per_mod):
    t, d = x.shape
    tm = MERGE_TM
    tok = lambda width: pl.BlockSpec((tm, width), lambda i: (i, 0))
    weights = [w["w_out_a"], w["w_out_b"], w["w_out_c"], w["w_pool"], w["pool_scale"], w["w_o"], w["norm_ffn_g"]]
    return pl.pallas_call(
        _merge_kernel,
        grid=(t // tm,),
        in_specs=[tok(d)] + [tok(W_BR)] * 4 + [tok(d), _mod_spec(tm, tokens_per_mod, mod_row),
                                               pl.BlockSpec(w_gate.shape, lambda i: (0, 0))]
                 + [_layer_spec(a, layer) for a in weights],
        out_specs=[tok(d), tok(d)],
        out_shape=[jax.ShapeDtypeStruct((t, d), F32), jax.ShapeDtypeStruct((t, d), BF16)],
        compiler_params=_cparams(),
        name="merge",
    )(h, *us, x, mod, w_gate, *weights)


def _ffn_kernel(h2_ref, x1_ref, mod_ref, w13_ref, w2_ref, gf_ref, o_ref, *, final):
    d = D_MODEL
    w13_ref, w2_ref = w13_ref.at[0], w2_ref.at[0]
    h2 = h2_ref[...]
    acc = None
    for lo in range(0, D_FF, FFN_CHUNK):
        hi = min(lo + FFN_CHUNK, D_FF)
        gate = _dot(h2, w13_ref[:, lo:hi])
        up = _dot(h2, w13_ref[:, D_FF + lo:D_FF + hi])
        part = _dot((_silu(gate) * up).astype(BF16), w2_ref[lo:hi, :])
        acc = part if acc is None else acc + part
    g_f = mod_ref[0][:, 5 * d:6 * d]
    x2 = x1_ref[...] + g_f * acc
    if final:
        x2 = _rmsnorm(x2, gf_ref[...])
    o_ref[...] = x2


def _ffn(h2, x1, mod, mod_row, w13, w2, final_g, layer, tokens_per_mod, final):
    t, d = x1.shape
    tm = FFN_TM
    return pl.pallas_call(
        functools.partial(_ffn_kernel, final=final),
        grid=(t // tm,),
        in_specs=[pl.BlockSpec((tm, d), lambda i: (i, 0)), pl.BlockSpec((tm, d), lambda i: (i, 0)),
                  _mod_spec(tm, tokens_per_mod, mod_row), _layer_spec(w13, layer), _layer_spec(w2, layer),
                  pl.BlockSpec(final_g.shape, lambda i: (0, 0))],
        out_specs=pl.BlockSpec((tm, d), lambda i: (i, 0)),
        out_shape=jax.ShapeDtypeStruct((t, d), F32),
        compiler_params=_cparams(),
        name="ffn",
    )(h2, x1, mod, w13, w2, final_g)


def _block_diag_heads(s):
    eye = jnp.eye(H_B, dtype=s.dtype)
    bd = jnp.einsum("...hab,hg->...hagb", s, eye)
    return bd.reshape(s.shape[:-3] + (H_B * s.shape[-2], H_B * s.shape[-1]))


def kernel(x_prompt, x_sample, state_hgrn, c, c_ctx, ada_w, ada_b, norm_mix_g, w_in, conv_a_w, conv_a_b, ln_a_g,
           ln_a_b, w_out_a, hgrn_lb_logits, hgrn_norm_g, w_out_b, conv_c_w, w_out_c, pool_w, pool_scale, w_o,
           norm_ffn_g, ffn_w13, ffn_w2, final_norm_g):
    depth = w_in.shape[0]
    d = D_MODEL
    n_ctx, n_lat = x_prompt.shape[0], x_sample.shape[0]
    t_ctx, t_lat = n_ctx * CTX_SEQ, n_lat * LAT_SEQ

    n_rows = -(-(1 + n_lat) // 8) * 8
    c_rows = jnp.zeros((n_rows, d), F32).at[0].set(c_ctx).at[1:1 + n_lat].set(c)
    mod = _modulation(c_rows, ada_w, ada_b).reshape(depth * n_rows, 1, 6 * d)

    lb_all = jnp.cumsum(jax.nn.softmax(hgrn_lb_logits.astype(F32), axis=0), axis=0)
    lb_all = (lb_all - lb_all[:1]).reshape(depth, 1, 2 * W_BR)

    head_mask = _block_diag_heads(jnp.ones((H_B, HGRN_HD, HGRN_HD), F32))
    level_masks = jnp.asarray(_level_masks())

    rows = lambda a: a.reshape(depth, 1, -1)
    w = {
        "w_out_a": w_out_a.astype(BF16), "w_out_b": w_out_b.astype(BF16), "w_out_c": w_out_c.astype(BF16),
        "w_pool": _block_diag_heads(pool_w).astype(BF16), "pool_scale": rows(pool_scale),
        "w_o": w_o.astype(BF16), "norm_ffn_g": rows(norm_ffn_g),
        "conv_a_w": conv_a_w, "conv_a_b": rows(conv_a_b), "ln_a_g": rows(ln_a_g), "ln_a_b": rows(ln_a_b),
        "conv_c_w": conv_c_w,
    }
    norm_mix, hgrn_g = rows(norm_mix_g), rows(hgrn_norm_g)
    w13 = ffn_w13.astype(BF16)
    w2 = ffn_w2.astype(BF16)
    final_g = final_norm_g.reshape(1, d)

    y_p = x_prompt.reshape(t_ctx, d)
    y_s = x_sample.reshape(t_lat, d)
    ctx_states = []
    for l in range(depth):
        w_mix = w_in[l, :, :N_MIX].astype(BF16)
        w_gate = w_in[l, :, N_MIX:].astype(BF16)
        final = l == depth - 1
        outs = []
        for latent, x, tokens_per_mod in ((False, y_p, t_ctx), (True, y_s, LAT_SEQ)):
            mod_row = l * n_rows + (1 if latent else 0)
            h, z = _inproj(x, mod, mod_row, norm_mix, w_mix, lb_all, l, tokens_per_mod)
            u_a, u_c, u_d = _mixers(z, w, l, latent)
            hg = _hgrn(z, hgrn_g, head_mask, level_masks, state_hgrn if latent else None, l,
                       LAT_SEQ if latent else CTX_SEQ)
            if not latent:
                ctx_states.append(hg[1])
            x1, h2 = _merge(h, (u_a, hg[0], u_c, u_d), x, mod, mod_row, w_gate, w, l, tokens_per_mod)
            outs.append(_ffn(h2, x1, mod, mod_row, w13, w2, final_g, l, tokens_per_mod, final))
        y_p, y_s = outs

    new_state = jnp.stack(ctx_states, axis=1).astype(x_prompt.dtype)
    return (y_p.reshape(x_prompt.shape), y_s.reshape(x_sample.shape), new_state)
```

```python
import functools

import numpy as np
import jax
import jax.numpy as jnp
from jax import lax
from jax.experimental import pallas as pl
from jax.experimental.pallas import tpu as pltpu

F32 = jnp.float32
BF16 = jnp.bfloat16

D_MODEL = 1024
GRID_W = 64
EPS = 1e-6
W_BR = 256
CONV_A_K = 31
H_B = 4
HGRN_HD = 64
POOL_WINDOWS = (2, 4, 8, 16)
POOL_GC = 64
D_FF = 2816
N_MIX = 11 * W_BR
N_Z = 9 * W_BR
CTX_SEQ = 256
LAT_SEQ = 2048

INPROJ_TM = 1024
MERGE_TM = 256
FFN_TM = 1024
FFN_CHUNK = 256
INPROJ_SUB = 128
MERGE_SUB = 256

VMEM_LIMIT = 56 * 1024 * 1024
LOG2E = 1.4426950408889634


def _cparams(n_axes=1):
    return pltpu.CompilerParams(dimension_semantics=("arbitrary",) * n_axes, vmem_limit_bytes=VMEM_LIMIT)


def _sigmoid(x):
    return 1.0 / (1.0 + jnp.exp(-x))


def _silu(x):
    return x * _sigmoid(x)


def _rmsnorm(x, g):
    return x * lax.rsqrt(jnp.mean(x * x, axis=-1, keepdims=True) + EPS) * g


def _dot(a, b):
    return jnp.dot(a, b, preferred_element_type=F32)


def _layer_spec(a, layer):
    return pl.BlockSpec((1,) + a.shape[1:], lambda i: (layer,) + (0,) * (a.ndim - 1))


def _mod_spec(tm, tokens_per_mod, first_row):
    return pl.BlockSpec((1, 1, 6 * D_MODEL), lambda i: (first_row + (i * tm) // tokens_per_mod, 0, 0))


def _mod_kernel(c_ref, w_ref, b_ref, o_ref):
    s = _silu(c_ref[...]).astype(BF16)
    o_ref[0] = _dot(s, w_ref[0].astype(BF16)) + b_ref[0]


def _modulation(c_rows, ada_w, ada_b):
    depth, d, n = ada_w.shape
    nr = c_rows.shape[0]
    tn = 768
    return pl.pallas_call(
        _mod_kernel,
        grid=(depth, n // tn),
        in_specs=[
            pl.BlockSpec((nr, d), lambda l, j: (0, 0)),
            pl.BlockSpec((1, d, tn), lambda l, j: (l, 0, j)),
            pl.BlockSpec((1, 1, tn), lambda l, j: (l, 0, j)),
        ],
        out_specs=pl.BlockSpec((1, nr, tn), lambda l, j: (l, 0, j)),
        out_shape=jax.ShapeDtypeStruct((depth, nr, n), F32),
        compiler_params=_cparams(2),
        name="modulation",
    )(c_rows, ada_w, ada_b.reshape(depth, 1, n))


def _log2_forget(z, lb):
    zs = z * LOG2E
    ls = jnp.minimum(zs, 0.0) - jnp.log2(1.0 + jnp.exp2(-jnp.abs(zs)))
    a = jnp.log2(lb)
    b = jnp.log1p(-lb) * LOG2E + ls
    return jnp.maximum(a, b) + jnp.log2(1.0 + jnp.exp2(-jnp.abs(a - b)))


def _inproj_kernel(x_ref, mod_ref, g_ref, w_ref, lb_ref, h_ref, z_ref):
    d, w = D_MODEL, W_BR
    mod = mod_ref[0]
    sh, sc = mod[:, 0:d], mod[:, d:2 * d]
    lb = lb_ref[0]
    for r0 in range(0, x_ref.shape[0], INPROJ_SUB):
        rows = slice(r0, r0 + INPROJ_SUB)
        h = (_rmsnorm(x_ref[rows, :], g_ref[0]) * (1.0 + sc) + sh).astype(BF16)
        h_ref[rows, :] = h

        def proj(j, n, h=h):
            return _dot(h, w_ref[:, j * w:(j + n) * w])

        za = proj(0, 2)
        z_ref[rows, 0 * w:1 * w] = za[:, 0:w] * _sigmoid(za[:, w:2 * w])
        zb = proj(2, 3)
        z_ref[rows, 1 * w:2 * w] = zb[:, 0:w]
        z_ref[rows, 2 * w:4 * w] = _log2_forget(zb[:, w:3 * w], lb)
        zv = proj(5, 2)
        z_ref[rows, 4 * w:5 * w] = zv[:, 0:w]
        z_ref[rows, 5 * w:6 * w] = _silu(zv[:, w:2 * w])
        zc = proj(7, 3)
        z_ref[rows, 6 * w:7 * w] = zc[:, 0:w]
        z_ref[rows, 7 * w:8 * w] = zc[:, w:2 * w] * zc[:, 2 * w:3 * w]
        z_ref[rows, 8 * w:9 * w] = proj(10, 1)


def _inproj(x, mod, mod_row, g, w_mix, lb, layer, tokens_per_mod):
    t, d = x.shape
    tm = INPROJ_TM
    return pl.pallas_call(
        _inproj_kernel,
        grid=(t // tm,),
        in_specs=[
            pl.BlockSpec((tm, d), lambda i: (i, 0)),
            _mod_spec(tm, tokens_per_mod, mod_row),
            _layer_spec(g, layer),
            pl.BlockSpec((d, N_MIX), lambda i: (0, 0)),
            _layer_spec(lb, layer),
        ],
        out_specs=[
            pl.BlockSpec((tm, d), lambda i: (i, 0)),
            pl.BlockSpec((tm, N_Z), lambda i: (i, 0)),
        ],
        out_shape=[jax.ShapeDtypeStruct((t, d), BF16), jax.ShapeDtypeStruct((t, N_Z), F32)],
        compiler_params=_cparams(),
        name="inproj",
    )(x, mod, g, w_mix, lb)


def _row_index(rows):
    return lax.broadcasted_iota(jnp.int32, (rows, W_BR), 0)


def _shift_rows(x, j, pos, seg):
    if j == 0:
        return x
    y = pltpu.roll(x, (-j) % x.shape[0], axis=0)
    valid = (pos < seg - j) if j > 0 else (pos >= -j)
    return jnp.where(valid, y, 0.0)


def _conv_a_ln(x, pos, seg, wa_ref, ba_ref, lg_ref, lbeta_ref):
    half = CONV_A_K // 2
    acc = x * wa_ref[half:half + 1, :]
    for j in range(-half, half + 1):
        if j != 0:
            acc = acc + _shift_rows(x, j, pos, seg) * wa_ref[half + j:half + j + 1, :]
    u = acc + ba_ref[...]
    mu = jnp.mean(u, axis=-1, keepdims=True)
    var = jnp.mean(jnp.square(u - mu), axis=-1, keepdims=True)
    return _silu((u - mu) * lax.rsqrt(var + EPS) * lg_ref[...] + lbeta_ref[...])


LANE_TILE = 128


def _pool_lane_tiles():
    per_tile = LANE_TILE // POOL_GC
    tiles = []
    for i in range(W_BR // LANE_TILE):
        lane = lax.broadcasted_iota(jnp.int32, (1, LANE_TILE), 1) + i * LANE_TILE
        half = jnp.left_shift(1, lane // POOL_GC)
        reach = max(POOL_WINDOWS[i * per_tile:(i + 1) * per_tile]) // 2
        tiles.append((slice(i * LANE_TILE, (i + 1) * LANE_TILE), half, reach))
    return tiles


def _mix_ctx_kernel(ua_ref, cb_ref, cc_ref, d_ref, wa_ref, ba_ref, lg_ref, lbeta_ref, wc_ref,
                    oa_ref, oc_ref, od_ref):
    wa_ref, ba_ref, lg_ref, lbeta_ref, wc_ref = (r.at[0] for r in (wa_ref, ba_ref, lg_ref, lbeta_ref, wc_ref))
    seg = CTX_SEQ
    pos = _row_index(seg)
    posl = lax.broadcasted_iota(jnp.int32, (seg, LANE_TILE), 0)
    tiles = _pool_lane_tiles()

    def body(s, carry):
        rows = pl.ds(pl.multiple_of(s * seg, seg), seg)
        oa_ref[rows, :] = _conv_a_ln(ua_ref[rows, :], pos, seg, wa_ref, ba_ref, lg_ref, lbeta_ref).astype(BF16)
        cc = cc_ref[rows, :]
        u = (_shift_rows(cc, -1, pos, seg) * wc_ref[0:1, :] + cc * wc_ref[1:2, :]
             + _shift_rows(cc, 1, pos, seg) * wc_ref[2:3, :])
        oc_ref[rows, :] = (cb_ref[rows, :] * u).astype(BF16)
        for lanes, half, reach in tiles:
            dd = d_ref[rows, lanes]
            acc = jnp.zeros_like(dd)
            cnt = jnp.zeros_like(dd)
            for j in range(-reach, reach):
                lane_on = (j >= -half) & (j <= half - 1)
                valid = (posl + j >= 0) & (posl + j < seg) & lane_on
                acc = acc + jnp.where(valid, _shift_rows(dd, j, posl, seg), 0.0)
                cnt = cnt + jnp.where(valid, 1.0, 0.0)
            od_ref[rows, lanes] = (acc / cnt - dd).astype(BF16)
        return carry

    lax.fori_loop(0, ua_ref.shape[0] // seg, body, 0)


def _mix_lat_kernel(ua_ref, cb_ref, cc_ref, d_ref, wa_ref, ba_ref, lg_ref, lbeta_ref, wc_ref,
                    oa_ref, oc_ref, od_ref):
    wa_ref, ba_ref, lg_ref, lbeta_ref, wc_ref = (r.at[0] for r in (wa_ref, ba_ref, lg_ref, lbeta_ref, wc_ref))
    seg = GRID_W
    n_rows = LAT_SEQ // GRID_W
    pos = _row_index(seg)
    tiles = _pool_lane_tiles()

    def grid_row(ref, r, j, lanes=slice(None)):
        rr = jnp.clip(r + j, 0, n_rows - 1)
        ok = jnp.where(jnp.logical_and(r + j >= 0, r + j < n_rows), 1.0, 0.0)
        return ref[pl.ds(pl.multiple_of(rr * seg, seg), seg), lanes], ok

    def body(r, carry):
        rows = pl.ds(pl.multiple_of(r * seg, seg), seg)
        oa_ref[rows, :] = _conv_a_ln(ua_ref[rows, :], pos, seg, wa_ref, ba_ref, lg_ref, lbeta_ref).astype(BF16)
        up, ok_up = grid_row(cc_ref, r, -1)
        dn, ok_dn = grid_row(cc_ref, r, 1)
        u = up * (wc_ref[0:1, :] * ok_up) + cc_ref[rows, :] * wc_ref[1:2, :] + dn * (wc_ref[2:3, :] * ok_dn)
        oc_ref[rows, :] = (cb_ref[rows, :] * u).astype(BF16)
        for lanes, half, reach in tiles:
            acc = jnp.zeros((seg, LANE_TILE), F32)
            cnt = jnp.zeros((1, LANE_TILE), F32)
            for j in range(-reach, reach):
                lane_on = jnp.where((j >= -half) & (j <= half - 1), 1.0, 0.0)
                x, ok = grid_row(d_ref, r, j, lanes)
                wgt = lane_on * ok
                acc = acc + x * wgt
                cnt = cnt + wgt
            od_ref[rows, lanes] = (acc / cnt - d_ref[rows, lanes]).astype(BF16)
        return carry

    lax.fori_loop(0, n_rows, body, 0)


def _mixers(z, p, layer, latent):
    t = z.shape[0]
    blk = LAT_SEQ
    zspec = lambda col: pl.BlockSpec((blk, W_BR), lambda i, col=col: (i, col))
    params = [p["conv_a_w"], p["conv_a_b"], p["ln_a_g"], p["ln_a_b"], p["conv_c_w"]]
    out = jax.ShapeDtypeStruct((t, W_BR), BF16)
    return pl.pallas_call(
        _mix_lat_kernel if latent else _mix_ctx_kernel,
        grid=(t // blk,),
        in_specs=[zspec(0), zspec(6), zspec(7), zspec(8)] + [_layer_spec(a, layer) for a in params],
        out_specs=[pl.BlockSpec((blk, W_BR), lambda i: (i, 0))] * 3,
        out_shape=[out, out, out],
        compiler_params=_cparams(),
        name="mix_lat" if latent else "mix_ctx",
    )(z, z, z, z, *params)


SB = 256
SUB = 8
LEVELS = (8, 16, 32, 64, 128)
NQ = SB // 2


def _level_spans(m, upper):
    off = m if upper else 0
    return [(i * 2 * m + off, i * 2 * m + off + m) for i in range(SB // (2 * m))]


def _take(x, spans):
    parts = [x[a:b] for a, b in spans]
    return parts[0] if len(parts) == 1 else jnp.concatenate(parts, axis=0)


def _level_masks():
    tq = np.arange(H_B * NQ)[:, None] % NQ
    ts = np.arange(NQ)[None, :]
    return np.stack([((tq // m) == (ts // m)) for m in LEVELS[:-1]]).astype(np.float32)


def _cumsum_rows(x):
    g = SB // SUB
    x3 = x.reshape(g, SUB, x.shape[-1])
    sub = lax.broadcasted_iota(jnp.int32, x3.shape, 1)
    step = 1
    while step < SUB:
        x3 = x3 + jnp.where(sub >= step, pltpu.roll(x3, step, axis=1), 0.0)
        step *= 2
    tot = x3[:, SUB - 1:SUB, :]
    offs = [jnp.zeros_like(tot[0:1])]
    for i in range(1, g):
        offs.append(offs[-1] + tot[i - 1:i])
    return (x3 + jnp.concatenate(offs, axis=0)).reshape(x.shape)


def _hgrn_kernel(*refs, has_s0):
    if has_s0:
        q_ref, lff_ref, lfb_ref, v_ref, sg_ref, ng_ref, hm_ref, lm_ref, s0_ref, o_ref, oacc, st = refs
        sfin_ref = None
    else:
        q_ref, lff_ref, lfb_ref, v_ref, sg_ref, ng_ref, hm_ref, lm_ref, o_ref, sfin_ref, oacc, st = refs
        s0_ref = None
    seq, w = q_ref.shape
    n_sb = seq // SB
    with_inter = has_s0 or n_sb > 1
    head_ones = hm_ref[...].astype(BF16)
    lane_head = lax.broadcasted_iota(jnp.int32, (1, w), 1) // HGRN_HD
    in_head = [lane_head == h for h in range(H_B)]
    sub3 = lax.broadcasted_iota(jnp.int32, (SB // SUB, SUB, w), 1)
    nt = (((1,), (1,)), ((), ()))
    tn = (((0,), (0,)), ((), ()))
    head_blocks = [slice(h * HGRN_HD, (h + 1) * HGRN_HD) for h in range(H_B)]

    for direction in range(2 if with_inter else 0):
        st[direction] = jnp.zeros((w, w), F32)
        if s0_ref is not None:
            for h, blk in enumerate(head_blocks):
                st[direction, blk, blk] = s0_ref[0, 0, direction, h]
            st[direction] = st[direction].T

    def sb_body(si, carry):
        for direction, lf_ref in ((0, lff_ref), (1, lfb_ref)):
            back = direction == 1
            base = pl.multiple_of(((n_sb - 1 - si) if back else si) * SB, SB)
            rows = pl.ds(base, SB)
            lf = lf_ref[rows, :]
            q = q_ref[rows, :]
            v = v_ref[rows, :]
            vb = v.astype(BF16)
            key = 1.0 - jnp.exp2(lf)
            g = _cumsum_rows(lf)
            tot = g[SB - 1:SB, :]
            b = (lf - g) if back else g

            q3, k3, b3, v3 = (x.reshape(SB // SUB, SUB, w) for x in (q, key, b, v))
            acc = _dot((q * key).astype(BF16), head_ones) * v
            for dist in range(1, SUB):
                sft = (SUB - dist) if back else dist
                valid = (sub3 < SUB - dist) if back else (sub3 >= dist)
                e = jnp.where(valid, q3 * pltpu.roll(k3, sft, axis=1) * jnp.exp2(b3 - pltpu.roll(b3, sft, axis=1)), 0.0)
                red = _dot(e.reshape(SB, w).astype(BF16), head_ones)
                acc = acc + red * pltpu.roll(v3, sft, axis=1).reshape(SB, w)
            oacc[direction, rows, :] = acc

            for li, m in enumerate(LEVELS):
                lower, upper = _level_spans(m, False), _level_spans(m, True)
                qs, ks = (lower, upper) if back else (upper, lower)
                edge = [(a + m) if back else (a + m - 1) for a, _ in lower]
                bm = jnp.concatenate([jnp.broadcast_to(b[r:r + 1], (m, w)) for r in edge], axis=0)
                qg = _take(q, qs) * jnp.exp2(_take(b, qs) - bm)
                kg = (_take(key, ks) * jnp.exp2(bm - _take(b, ks))).astype(BF16)
                stacked = jnp.concatenate([jnp.where(in_head[h], qg, 0.0) for h in range(H_B)], axis=0)
                p = lax.dot_general(stacked.astype(BF16), kg, nt, preferred_element_type=F32)
                if m < NQ:
                    p = p * lm_ref[li]
                r = _dot(p.astype(BF16), _take(vb, ks))
                res = jnp.where(in_head[0], r[0:NQ], 0.0)
                for h in range(1, H_B):
                    res = res + jnp.where(in_head[h], r[h * NQ:(h + 1) * NQ], 0.0)
                for bi, (a0, _) in enumerate(qs):
                    dst = pl.ds(base + a0, m)
                    oacc[direction, dst, :] = oacc[direction, dst, :] + res[bi * m:(bi + 1) * m]

            cq, ck = (tot, 0.0) if back else (0.0, tot)
            kp = (key * jnp.exp2(ck - b)).astype(BF16)
            if with_inter:
                s_old = st[direction]
                qp = (q * jnp.exp2(b + cq)).astype(BF16)
                oacc[direction, rows, :] = oacc[direction, rows, :] + lax.dot_general(
                    qp, s_old.astype(BF16), nt, preferred_element_type=F32)
                kv_t = lax.dot_general(vb, kp, tn, preferred_element_type=F32)
                st[direction] = s_old * jnp.exp2(tot) + kv_t * hm_ref[...]
            else:
                kv = lax.dot_general(kp, vb, tn, preferred_element_type=F32)
                for h, blk in enumerate(head_blocks):
                    sfin_ref[0, direction, h] = kv[blk, blk]
        return carry

    lax.fori_loop(0, n_sb, sb_body, 0)
    if sfin_ref is not None and with_inter:
        for direction in range(2):
            s_kv = st[direction].T
            for h, blk in enumerate(head_blocks):
                sfin_ref[0, direction, h] = s_kv[blk, blk]

    def out_body(i, carry):
        rows = pl.ds(pl.multiple_of(i * SB, SB), SB)
        o = oacc[0, rows, :] + oacc[1, rows, :]
        o2 = o * o
        hi = o2.astype(BF16)
        lo = (o2 - hi.astype(F32)).astype(BF16)
        ms = (_dot(hi, head_ones) + _dot(lo, head_ones)) * (1.0 / HGRN_HD)
        o_ref[rows, :] = (o * lax.rsqrt(ms + EPS) * ng_ref[0] * sg_ref[rows, :]).astype(BF16)
        return carry

    lax.fori_loop(0, n_sb, out_body, 0)


def _hgrn(z, norm_g, head_mask, level_masks, s0, layer, seq):
    t = z.shape[0]
    n_seq = t // seq
    zspec = lambda col: pl.BlockSpec((seq, W_BR), lambda i, col=col: (i, col))
    wb = H_B * HGRN_HD
    state_block = (1, 2, H_B, HGRN_HD, HGRN_HD)
    in_specs = [zspec(1), zspec(2), zspec(3), zspec(4), zspec(5),
                _layer_spec(norm_g, layer),
                pl.BlockSpec((wb, wb), lambda i: (0, 0)),
                pl.BlockSpec(level_masks.shape, lambda i: (0, 0, 0))]
    args = [z, z, z, z, z, norm_g, head_mask, level_masks]
    out_specs = [pl.BlockSpec((seq, wb), lambda i: (i, 0))]
    out_shape = [jax.ShapeDtypeStruct((t, wb), BF16)]
    if s0 is not None:
        in_specs.append(pl.BlockSpec((1,) + state_block, lambda i: (i, layer, 0, 0, 0, 0)))
        args.append(s0)
    else:
        out_specs.append(pl.BlockSpec(state_block, lambda i: (i, 0, 0, 0, 0)))
        out_shape.append(jax.ShapeDtypeStruct((n_seq, 2, H_B, HGRN_HD, HGRN_HD), F32))
    return pl.pallas_call(
        functools.partial(_hgrn_kernel, has_s0=s0 is not None),
        grid=(n_seq,),
        in_specs=in_specs,
        out_specs=out_specs,
        out_shape=out_shape,
        scratch_shapes=[pltpu.VMEM((2, seq, wb), F32), pltpu.VMEM((2, wb, wb), F32)],
        compiler_params=_cparams(),
        name="hgrn_lat" if s0 is not None else "hgrn_ctx",
    )(*args)


def _merge_kernel(h_ref, ua_ref, ub_ref, uc_ref, ud_ref, x_ref, mod_ref, wg_ref, wa_ref, wb_ref, wc_ref, wd_ref,
                  ps_ref, wo_ref, g2_ref, x1_ref, h2_ref):
    d = D_MODEL
    mod = mod_ref[0]
    g_m, sh_f, sc_f = mod[:, 2 * d:3 * d], mod[:, 3 * d:4 * d], mod[:, 4 * d:5 * d]
    for r0 in range(0, x_ref.shape[0], MERGE_SUB):
        rows = slice(r0, r0 + MERGE_SUB)
        h = h_ref[rows, :]
        merged = None
        for i, (u_ref, w_ref) in enumerate(((ua_ref, wa_ref), (ub_ref, wb_ref), (uc_ref, wc_ref), (ud_ref, wd_ref))):
            y = _dot(u_ref[rows, :], w_ref[0])
            if i == 3:
                y = y * ps_ref[0]
            term = _sigmoid(_dot(h, wg_ref[:, i * d:(i + 1) * d])) * y
            merged = term if merged is None else merged + term
        mix = _dot(merged.astype(BF16), wo_ref[0])
        x1 = x_ref[rows, :] + g_m * mix
        x1_ref[rows, :] = x1
        h2_ref[rows, :] = (_rmsnorm(x1, g2_ref[0]) * (1.0 + sc_f) + sh_f).astype(BF16)


def _merge(h, us, x, mod, mod_row, w_gate, w, layer, tokens_per_mod):
    t, d = x.shape
    tm = MERGE_TM
    tok = lambda width: pl.BlockSpec((tm, width), lambda i: (i, 0))
    weights = [w["w_out_a"], w["w_out_b"], w["w_out_c"], w["w_pool"], w["pool_scale"], w["w_o"], w["norm_ffn_g"]]
    return pl.pallas_call(
        _merge_kernel,
        grid=(t // tm,),
        in_specs=[tok(d)] + [tok(W_BR)] * 4 + [tok(d), _mod_spec(tm, tokens_per_mod, mod_row),
                                               pl.BlockSpec(w_gate.shape, lambda i: (0, 0))]
                 + [_layer_spec(a, layer) for a in weights],
        out_specs=[tok(d), tok(d)],
        out_shape=[jax.ShapeDtypeStruct((t, d), F32), jax.ShapeDtypeStruct((t, d), BF16)],
        compiler_params=_cparams(),
        name="merge",
    )(h, *us, x, mod, w_gate, *weights)


def _ffn_kernel(h2_ref, x1_ref, mod_ref, w13_ref, w2_ref, gf_ref, o_ref, *, final):
    d = D_MODEL
    w13_ref, w2_ref = w13_ref.at[0], w2_ref.at[0]
    h2 = h2_ref[...]
    acc = None
    for lo in range(0, D_FF, FFN_CHUNK):
        hi = min(lo + FFN_CHUNK, D_FF)
        gate = _dot(h2, w13_ref[:, lo:hi])
        up = _dot(h2, w13_ref[:, D_FF + lo:D_FF + hi])
        part = _dot((_silu(gate) * up).astype(BF16), w2_ref[lo:hi, :])
        acc = part if acc is None else acc + part
    g_f = mod_ref[0][:, 5 * d:6 * d]
    x2 = x1_ref[...] + g_f * acc
    if final:
        x2 = _rmsnorm(x2, gf_ref[...])
    o_ref[...] = x2


def _ffn(h2, x1, mod, mod_row, w13, w2, final_g, layer, tokens_per_mod, final):
    t, d = x1.shape
    tm = FFN_TM
    return pl.pallas_call(
        functools.partial(_ffn_kernel, final=final),
        grid=(t // tm,),
        in_specs=[pl.BlockSpec((tm, d), lambda i: (i, 0)), pl.BlockSpec((tm, d), lambda i: (i, 0)),
                  _mod_spec(tm, tokens_per_mod, mod_row), _layer_spec(w13, layer), _layer_spec(w2, layer),
                  pl.BlockSpec(final_g.shape, lambda i: (0, 0))],
        out_specs=pl.BlockSpec((tm, d), lambda i: (i, 0)),
        out_shape=jax.ShapeDtypeStruct((t, d), F32),
        compiler_params=_cparams(),
        name="ffn",
    )(h2, x1, mod, w13, w2, final_g)


def _block_diag_heads(s):
    eye = jnp.eye(H_B, dtype=s.dtype)
    bd = jnp.einsum("...hab,hg->...hagb", s, eye)
    return bd.reshape(s.shape[:-3] + (H_B * s.shape[-2], H_B * s.shape[-1]))


def kernel(x_prompt, x_sample, state_hgrn, c, c_ctx, ada_w, ada_b, norm_mix_g, w_in, conv_a_w, conv_a_b, ln_a_g,
           ln_a_b, w_out_a, hgrn_lb_logits, hgrn_norm_g, w_out_b, conv_c_w, w_out_c, pool_w, pool_scale, w_o,
           norm_ffn_g, ffn_w13, ffn_w2, final_norm_g):
    depth = w_in.shape[0]
    d = D_MODEL
    n_ctx, n_lat = x_prompt.shape[0], x_sample.shape[0]
    t_ctx, t_lat = n_ctx * CTX_SEQ, n_lat * LAT_SEQ

    n_rows = -(-(1 + n_lat) // 8) * 8
    c_rows = jnp.zeros((n_rows, d), F32).at[0].set(c_ctx).at[1:1 + n_lat].set(c)
    mod = _modulation(c_rows, ada_w, ada_b).reshape(depth * n_rows, 1, 6 * d)

    lb_all = jnp.cumsum(jax.nn.softmax(hgrn_lb_logits.astype(F32), axis=0), axis=0)
    lb_all = (lb_all - lb_all[:1]).reshape(depth, 1, 2 * W_BR)

    head_mask = _block_diag_heads(jnp.ones((H_B, HGRN_HD, HGRN_HD), F32))
    level_masks = jnp.asarray(_level_masks())

    rows = lambda a: a.reshape(depth, 1, -1)
    w = {
        "w_out_a": w_out_a.astype(BF16), "w_out_b": w_out_b.astype(BF16), "w_out_c": w_out_c.astype(BF16),
        "w_pool": _block_diag_heads(pool_w).astype(BF16), "pool_scale": rows(pool_scale),
        "w_o": w_o.astype(BF16), "norm_ffn_g": rows(norm_ffn_g),
        "conv_a_w": conv_a_w, "conv_a_b": rows(conv_a_b), "ln_a_g": rows(ln_a_g), "ln_a_b": rows(ln_a_b),
        "conv_c_w": conv_c_w,
    }
    norm_mix, hgrn_g = rows(norm_mix_g), rows(hgrn_norm_g)
    w13 = ffn_w13.astype(BF16)
    w2 = ffn_w2.astype(BF16)
    final_g = final_norm_g.reshape(1, d)

    y_p = x_prompt.reshape(t_ctx, d)
    y_s = x_sample.reshape(t_lat, d)
    ctx_states = []
    for l in range(depth):
        w_mix = w_in[l, :, :N_MIX].astype(BF16)
        w_gate = w_in[l, :, N_MIX:].astype(BF16)
        final = l == depth - 1
        outs = []
        for latent, x, tokens_per_mod in ((False, y_p, t_ctx), (True, y_s, LAT_SEQ)):
            mod_row = l * n_rows + (1 if latent else 0)
            h, z = _inproj(x, mod, mod_row, norm_mix, w_mix, lb_all, l, tokens_per_mod)
            u_a, u_c, u_d = _mixers(z, w, l, latent)
            hg = _hgrn(z, hgrn_g, head_mask, level_masks, state_hgrn if latent else None, l,
                       LAT_SEQ if latent else CTX_SEQ)
            if not latent:
                ctx_states.append(hg[1])
            x1, h2 = _merge(h, (u_a, hg[0], u_c, u_d), x, mod, mod_row, w_gate, w, l, tokens_per_mod)
            outs.append(_ffn(h2, x1, mod, mod_row, w13, w2, final_g, l, tokens_per_mod, final))
        y_p, y_s = outs

    new_state = jnp.stack(ctx_states, axis=1).astype(x_prompt.dtype)
    return (y_p.reshape(x_prompt.shape), y_s.reshape(x_sample.shape), new_state)
```

```python
import functools

import numpy as np
import jax
import jax.numpy as jnp
from jax import lax
from jax.experimental import pallas as pl
from jax.experimental.pallas import tpu as pltpu

F32 = jnp.float32
BF16 = jnp.bfloat16

D_MODEL = 1024
GRID_W = 64
EPS = 1e-6
W_BR = 256
CONV_A_K = 31
H_B = 4
HGRN_HD = 64
POOL_WINDOWS = (2, 4, 8, 16)
POOL_GC = 64
D_FF = 2816
N_MIX = 11 * W_BR
N_Z = 9 * W_BR
CTX_SEQ = 256
LAT_SEQ = 2048

INPROJ_TM = 1024
MERGE_TM = 512
FFN_TM = 1024
FFN_CHUNK = 256
INPROJ_SUB = 128
MERGE_SUB = 256
FFN_SUB = 512

VMEM_LIMIT = 56 * 1024 * 1024
LOG2E = 1.4426950408889634


def _cparams(n_axes=1):
    return pltpu.CompilerParams(dimension_semantics=("arbitrary",) * n_axes, vmem_limit_bytes=VMEM_LIMIT)


def _sigmoid(x):
    return 1.0 / (1.0 + jnp.exp(-x))


def _silu(x):
    return x * _sigmoid(x)


def _rmsnorm(x, g):
    return x * lax.rsqrt(jnp.mean(x * x, axis=-1, keepdims=True) + EPS) * g


def _dot(a, b):
    return jnp.dot(a, b, preferred_element_type=F32)


def _layer_spec(a, layer):
    return pl.BlockSpec((1,) + a.shape[1:], lambda i: (layer,) + (0,) * (a.ndim - 1))


def _mod_spec(tm, tokens_per_mod, first_row):
    return pl.BlockSpec((1, 1, 6 * D_MODEL), lambda i: (first_row + (i * tm) // tokens_per_mod, 0, 0))


def _mod_kernel(c_ref, w_ref, b_ref, o_ref):
    s = _silu(c_ref[...]).astype(BF16)
    o_ref[0] = _dot(s, w_ref[0].astype(BF16)) + b_ref[0]


def _modulation(c_rows, ada_w, ada_b):
    depth, d, n = ada_w.shape
    nr = c_rows.shape[0]
    tn = 1536
    return pl.pallas_call(
        _mod_kernel,
        grid=(depth, n // tn),
        in_specs=[
            pl.BlockSpec((nr, d), lambda l, j: (0, 0)),
            pl.BlockSpec((1, d, tn), lambda l, j: (l, 0, j)),
            pl.BlockSpec((1, 1, tn), lambda l, j: (l, 0, j)),
        ],
        out_specs=pl.BlockSpec((1, nr, tn), lambda l, j: (l, 0, j)),
        out_shape=jax.ShapeDtypeStruct((depth, nr, n), F32),
        compiler_params=_cparams(2),
        name="modulation",
    )(c_rows, ada_w, ada_b.reshape(depth, 1, n))


def _log2_forget(z, lb):
    zs = z * LOG2E
    ls = jnp.minimum(zs, 0.0) - jnp.log2(1.0 + jnp.exp2(-jnp.abs(zs)))
    a = jnp.log2(lb)
    b = jnp.log1p(-lb) * LOG2E + ls
    return jnp.maximum(a, b) + jnp.log2(1.0 + jnp.exp2(-jnp.abs(a - b)))


def _inproj_kernel(x_ref, mod_ref, g_ref, w_ref, lb_ref, h_ref, z_ref):
    d, w = D_MODEL, W_BR
    mod = mod_ref[0]
    sh, sc = mod[:, 0:d], mod[:, d:2 * d]
    lb = lb_ref[0]
    for r0 in range(0, x_ref.shape[0], INPROJ_SUB):
        rows = slice(r0, r0 + INPROJ_SUB)
        h = (_rmsnorm(x_ref[rows, :], g_ref[0]) * (1.0 + sc) + sh).astype(BF16)
        h_ref[rows, :] = h

        def proj(j, n, h=h):
            return _dot(h, w_ref[:, j * w:(j + n) * w])

        za = proj(0, 2)
        z_ref[rows, 0 * w:1 * w] = za[:, 0:w] * _sigmoid(za[:, w:2 * w])
        zb = proj(2, 3)
        z_ref[rows, 1 * w:2 * w] = zb[:, 0:w]
        z_ref[rows, 2 * w:4 * w] = _log2_forget(zb[:, w:3 * w], lb)
        zv = proj(5, 2)
        z_ref[rows, 4 * w:5 * w] = zv[:, 0:w]
        z_ref[rows, 5 * w:6 * w] = _silu(zv[:, w:2 * w])
        zc = proj(7, 3)
        z_ref[rows, 6 * w:7 * w] = zc[:, 0:w]
        z_ref[rows, 7 * w:8 * w] = zc[:, w:2 * w] * zc[:, 2 * w:3 * w]
        z_ref[rows, 8 * w:9 * w] = proj(10, 1)


def _inproj(x, mod, mod_row, g, w_mix, lb, layer, tokens_per_mod):
    t, d = x.shape
    tm = INPROJ_TM
    return pl.pallas_call(
        _inproj_kernel,
        grid=(t // tm,),
        in_specs=[
            pl.BlockSpec((tm, d), lambda i: (i, 0)),
            _mod_spec(tm, tokens_per_mod, mod_row),
            _layer_spec(g, layer),
            pl.BlockSpec((d, N_MIX), lambda i: (0, 0)),
            _layer_spec(lb, layer),
        ],
        out_specs=[
            pl.BlockSpec((tm, d), lambda i: (i, 0)),
            pl.BlockSpec((tm, N_Z), lambda i: (i, 0)),
        ],
        out_shape=[jax.ShapeDtypeStruct((t, d), BF16), jax.ShapeDtypeStruct((t, N_Z), F32)],
        compiler_params=_cparams(),
        name="inproj",
    )(x, mod, g, w_mix, lb)


def _row_index(rows):
    return lax.broadcasted_iota(jnp.int32, (rows, W_BR), 0)


def _shift_rows(x, j, pos, seg):
    if j == 0:
        return x
    y = pltpu.roll(x, (-j) % x.shape[0], axis=0)
    valid = (pos < seg - j) if j > 0 else (pos >= -j)
    return jnp.where(valid, y, 0.0)


def _conv_a_ln(x, wa_ref, ba_ref, lg_ref, lbeta_ref):
    half = CONV_A_K // 2
    seg = x.shape[0]
    pad = 2 * SUBLANES
    zeros = jnp.zeros((pad, x.shape[1]), F32)
    xpad = jnp.concatenate([zeros, x, zeros], axis=0)
    n = seg + 2 * pad
    rotated = [xpad] + [pltpu.roll(xpad, n - r, axis=0) for r in range(1, SUBLANES)]
    acc = None
    for j in range(-half, half + 1):
        whole, r = divmod(pad + j, SUBLANES)
        term = rotated[r][whole * SUBLANES:whole * SUBLANES + seg] * wa_ref[half + j:half + j + 1, :]
        acc = term if acc is None else acc + term
    u = acc + ba_ref[...]
    mu = jnp.mean(u, axis=-1, keepdims=True)
    var = jnp.mean(jnp.square(u - mu), axis=-1, keepdims=True)
    return _silu((u - mu) * lax.rsqrt(var + EPS) * lg_ref[...] + lbeta_ref[...])


LANE_TILE = 128
SUBLANES = 8


def _pool_lane_tiles():
    per_tile = LANE_TILE // POOL_GC
    tiles = []
    for i in range(W_BR // LANE_TILE):
        lane = lax.broadcasted_iota(jnp.int32, (1, LANE_TILE), 1) + i * LANE_TILE
        half = jnp.left_shift(1, lane // POOL_GC)
        reach = max(POOL_WINDOWS[i * per_tile:(i + 1) * per_tile]) // 2
        tiles.append((slice(i * LANE_TILE, (i + 1) * LANE_TILE), half, reach))
    return tiles


def _mix_ctx_kernel(ua_ref, cb_ref, cc_ref, d_ref, wa_ref, ba_ref, lg_ref, lbeta_ref, wc_ref,
                    oa_ref, oc_ref, od_ref):
    wa_ref, ba_ref, lg_ref, lbeta_ref, wc_ref = (r.at[0] for r in (wa_ref, ba_ref, lg_ref, lbeta_ref, wc_ref))
    seg = CTX_SEQ
    pos = _row_index(seg)
    posl = lax.broadcasted_iota(jnp.int32, (seg, LANE_TILE), 0)
    tiles = _pool_lane_tiles()

    def body(s, carry):
        rows = pl.ds(pl.multiple_of(s * seg, seg), seg)
        oa_ref[rows, :] = _conv_a_ln(ua_ref[rows, :], wa_ref, ba_ref, lg_ref, lbeta_ref).astype(BF16)
        cc = cc_ref[rows, :]
        u = (_shift_rows(cc, -1, pos, seg) * wc_ref[0:1, :] + cc * wc_ref[1:2, :]
             + _shift_rows(cc, 1, pos, seg) * wc_ref[2:3, :])
        oc_ref[rows, :] = (cb_ref[rows, :] * u).astype(BF16)
        for lanes, half, reach in tiles:
            dd = d_ref[rows, lanes]
            acc = jnp.zeros_like(dd)
            cnt = jnp.zeros_like(dd)
            for j in range(-reach, reach):
                lane_on = (j >= -half) & (j <= half - 1)
                valid = (posl + j >= 0) & (posl + j < seg) & lane_on
                acc = acc + jnp.where(valid, _shift_rows(dd, j, posl, seg), 0.0)
                cnt = cnt + jnp.where(valid, 1.0, 0.0)
            od_ref[rows, lanes] = (acc / cnt - dd).astype(BF16)
        return carry

    lax.fori_loop(0, ua_ref.shape[0] // seg, body, 0)


def _mix_lat_kernel(ua_ref, cb_ref, cc_ref, d_ref, wa_ref, ba_ref, lg_ref, lbeta_ref, wc_ref,
                    oa_ref, oc_ref, od_ref):
    wa_ref, ba_ref, lg_ref, lbeta_ref, wc_ref = (r.at[0] for r in (wa_ref, ba_ref, lg_ref, lbeta_ref, wc_ref))
    seg = GRID_W
    n_rows = LAT_SEQ // GRID_W
    pos = _row_index(seg)
    tiles = _pool_lane_tiles()

    def grid_row(ref, r, j, lanes=slice(None)):
        rr = jnp.clip(r + j, 0, n_rows - 1)
        ok = jnp.where(jnp.logical_and(r + j >= 0, r + j < n_rows), 1.0, 0.0)
        return ref[pl.ds(pl.multiple_of(rr * seg, seg), seg), lanes], ok

    def body(r, carry):
        rows = pl.ds(pl.multiple_of(r * seg, seg), seg)
        oa_ref[rows, :] = _conv_a_ln(ua_ref[rows, :], wa_ref, ba_ref, lg_ref, lbeta_ref).astype(BF16)
        up, ok_up = grid_row(cc_ref, r, -1)
        dn, ok_dn = grid_row(cc_ref, r, 1)
        u = up * (wc_ref[0:1, :] * ok_up) + cc_ref[rows, :] * wc_ref[1:2, :] + dn * (wc_ref[2:3, :] * ok_dn)
        oc_ref[rows, :] = (cb_ref[rows, :] * u).astype(BF16)
        for lanes, half, reach in tiles:
            acc = jnp.zeros((seg, LANE_TILE), F32)
            cnt = jnp.zeros((1, LANE_TILE), F32)
            for j in range(-reach, reach):
                lane_on = jnp.where((j >= -half) & (j <= half - 1), 1.0, 0.0)
                x, ok = grid_row(d_ref, r, j, lanes)
                wgt = lane_on * ok
                acc = acc + x * wgt
                cnt = cnt + wgt
            od_ref[rows, lanes] = (acc / cnt - d_ref[rows, lanes]).astype(BF16)
        return carry

    lax.fori_loop(0, n_rows, body, 0)


def _mixers(z, p, layer, latent):
    t = z.shape[0]
    blk = LAT_SEQ
    zspec = lambda col: pl.BlockSpec((blk, W_BR), lambda i, col=col: (i, col))
    params = [p["conv_a_w"], p["conv_a_b"], p["ln_a_g"], p["ln_a_b"], p["conv_c_w"]]
    out = jax.ShapeDtypeStruct((t, W_BR), BF16)
    return pl.pallas_call(
        _mix_lat_kernel if latent else _mix_ctx_kernel,
        grid=(t // blk,),
        in_specs=[zspec(0), zspec(6), zspec(7), zspec(8)] + [_layer_spec(a, layer) for a in params],
        out_specs=[pl.BlockSpec((blk, W_BR), lambda i: (i, 0))] * 3,
        out_shape=[out, out, out],
        compiler_params=_cparams(),
        name="mix_lat" if latent else "mix_ctx",
    )(z, z, z, z, *params)


SB = 256
SUB = 8
LEVELS = (8, 16, 32, 64, 128)
NQ = SB // 2


def _level_spans(m, upper):
    off = m if upper else 0
    return [(i * 2 * m + off, i * 2 * m + off + m) for i in range(SB // (2 * m))]


def _take(x, spans):
    parts = [x[a:b] for a, b in spans]
    return parts[0] if len(parts) == 1 else jnp.concatenate(parts, axis=0)


def _level_masks():
    tq = np.arange(H_B * NQ)[:, None] % NQ
    ts = np.arange(NQ)[None, :]
    return np.stack([((tq // m) == (ts // m)) for m in LEVELS[:-1]]).astype(np.float32)


def _cumsum_rows(x):
    g = SB // SUB
    x3 = x.reshape(g, SUB, x.shape[-1])
    sub = lax.broadcasted_iota(jnp.int32, x3.shape, 1)
    step = 1
    while step < SUB:
        x3 = x3 + jnp.where(sub >= step, pltpu.roll(x3, step, axis=1), 0.0)
        step *= 2
    tot = x3[:, SUB - 1:SUB, :]
    offs = [jnp.zeros_like(tot[0:1])]
    for i in range(1, g):
        offs.append(offs[-1] + tot[i - 1:i])
    return (x3 + jnp.concatenate(offs, axis=0)).reshape(x.shape)


def _hgrn_kernel(*refs, has_s0):
    if has_s0:
        q_ref, lff_ref, lfb_ref, v_ref, sg_ref, ng_ref, hm_ref, lm_ref, s0_ref, o_ref, oacc, st = refs
        sfin_ref = None
    else:
        q_ref, lff_ref, lfb_ref, v_ref, sg_ref, ng_ref, hm_ref, lm_ref, o_ref, sfin_ref, oacc, st = refs
        s0_ref = None
    seq, w = q_ref.shape
    n_sb = seq // SB
    with_inter = has_s0 or n_sb > 1
    head_ones = hm_ref[...].astype(BF16)
    lane_head = lax.broadcasted_iota(jnp.int32, (1, w), 1) // HGRN_HD
    in_head = [lane_head == h for h in range(H_B)]
    sub3 = lax.broadcasted_iota(jnp.int32, (SB // SUB, SUB, w), 1)
    nt = (((1,), (1,)), ((), ()))
    tn = (((0,), (0,)), ((), ()))
    head_blocks = [slice(h * HGRN_HD, (h + 1) * HGRN_HD) for h in range(H_B)]

    for direction in range(2 if with_inter else 0):
        st[direction] = jnp.zeros((w, w), F32)
        if s0_ref is not None:
            for h, blk in enumerate(head_blocks):
                st[direction, blk, blk] = s0_ref[0, 0, direction, h]
            st[direction] = st[direction].T

    def sb_body(si, carry):
        for direction, lf_ref in ((0, lff_ref), (1, lfb_ref)):
            back = direction == 1
            base = pl.multiple_of(((n_sb - 1 - si) if back else si) * SB, SB)
            rows = pl.ds(base, SB)
            lf = lf_ref[rows, :]
            q = q_ref[rows, :]
            v = v_ref[rows, :]
            vb = v.astype(BF16)
            key = 1.0 - jnp.exp2(lf)
            g = _cumsum_rows(lf)
            tot = g[SB - 1:SB, :]
            b = (lf - g) if back else g

            q3, k3, b3, v3 = (x.reshape(SB // SUB, SUB, w) for x in (q, key, b, v))
            acc = _dot((q * key).astype(BF16), head_ones) * v
            for dist in range(1, SUB):
                sft = (SUB - dist) if back else dist
                valid = (sub3 < SUB - dist) if back else (sub3 >= dist)
                e = jnp.where(valid, q3 * pltpu.roll(k3, sft, axis=1) * jnp.exp2(b3 - pltpu.roll(b3, sft, axis=1)), 0.0)
                red = _dot(e.reshape(SB, w).astype(BF16), head_ones)
                acc = acc + red * pltpu.roll(v3, sft, axis=1).reshape(SB, w)
            oacc[direction, rows, :] = acc

            for li, m in enumerate(LEVELS):
                lower, upper = _level_spans(m, False), _level_spans(m, True)
                qs, ks = (lower, upper) if back else (upper, lower)
                edge = [(a + m) if back else (a + m - 1) for a, _ in lower]
                bm = jnp.concatenate([jnp.broadcast_to(b[r:r + 1], (m, w)) for r in edge], axis=0)
                qg = _take(q, qs) * jnp.exp2(_take(b, qs) - bm)
                kg = (_take(key, ks) * jnp.exp2(bm - _take(b, ks))).astype(BF16)
                stacked = jnp.concatenate([jnp.where(in_head[h], qg, 0.0) for h in range(H_B)], axis=0)
                p = lax.dot_general(stacked.astype(BF16), kg, nt, preferred_element_type=F32)
                if m < NQ:
                    p = p * lm_ref[li]
                r = _dot(p.astype(BF16), _take(vb, ks))
                res = jnp.where(in_head[0], r[0:NQ], 0.0)
                for h in range(1, H_B):
                    res = res + jnp.where(in_head[h], r[h * NQ:(h + 1) * NQ], 0.0)
                for bi, (a0, _) in enumerate(qs):
                    dst = pl.ds(base + a0, m)
                    oacc[direction, dst, :] = oacc[direction, dst, :] + res[bi * m:(bi + 1) * m]

            cq, ck = (tot, 0.0) if back else (0.0, tot)
            kp = (key * jnp.exp2(ck - b)).astype(BF16)
            if with_inter:
                s_old = st[direction]
                qp = (q * jnp.exp2(b + cq)).astype(BF16)
                oacc[direction, rows, :] = oacc[direction, rows, :] + lax.dot_general(
                    qp, s_old.astype(BF16), nt, preferred_element_type=F32)
                kv_t = lax.dot_general(vb, kp, tn, preferred_element_type=F32)
                st[direction] = s_old * jnp.exp2(tot) + kv_t * hm_ref[...]
            else:
                kv = lax.dot_general(kp, vb, tn, preferred_element_type=F32)
                for h, blk in enumerate(head_blocks):
                    sfin_ref[0, direction, h] = kv[blk, blk]
        return carry

    lax.fori_loop(0, n_sb, sb_body, 0, unroll=2 if n_sb % 2 == 0 else 1)
    if sfin_ref is not None and with_inter:
        for direction in range(2):
            s_kv = st[direction].T
            for h, blk in enumerate(head_blocks):
                sfin_ref[0, direction, h] = s_kv[blk, blk]

    def out_body(i, carry):
        rows = pl.ds(pl.multiple_of(i * SB, SB), SB)
        o = oacc[0, rows, :] + oacc[1, rows, :]
        o2 = o * o
        hi = o2.astype(BF16)
        lo = (o2 - hi.astype(F32)).astype(BF16)
        ms = (_dot(hi, head_ones) + _dot(lo, head_ones)) * (1.0 / HGRN_HD)
        o_ref[rows, :] = (o * lax.rsqrt(ms + EPS) * ng_ref[0] * sg_ref[rows, :]).astype(BF16)
        return carry

    lax.fori_loop(0, n_sb, out_body, 0)


def _hgrn(z, norm_g, head_mask, level_masks, s0, layer, seq):
    t = z.shape[0]
    n_seq = t // seq
    zspec = lambda col: pl.BlockSpec((seq, W_BR), lambda i, col=col: (i, col))
    wb = H_B * HGRN_HD
    state_block = (1, 2, H_B, HGRN_HD, HGRN_HD)
    in_specs = [zspec(1), zspec(2), zspec(3), zspec(4), zspec(5),
                _layer_spec(norm_g, layer),
                pl.BlockSpec((wb, wb), lambda i: (0, 0)),
                pl.BlockSpec(level_masks.shape, lambda i: (0, 0, 0))]
    args = [z, z, z, z, z, norm_g, head_mask, level_masks]
    out_specs = [pl.BlockSpec((seq, wb), lambda i: (i, 0))]
    out_shape = [jax.ShapeDtypeStruct((t, wb), BF16)]
    if s0 is not None:
        in_specs.append(pl.BlockSpec((1,) + state_block, lambda i: (i, layer, 0, 0, 0, 0)))
        args.append(s0)
    else:
        out_specs.append(pl.BlockSpec(state_block, lambda i: (i, 0, 0, 0, 0)))
        out_shape.append(jax.ShapeDtypeStruct((n_seq, 2, H_B, HGRN_HD, HGRN_HD), F32))
    return pl.pallas_call(
        functools.partial(_hgrn_kernel, has_s0=s0 is not None),
        grid=(n_seq,),
        in_specs=in_specs,
        out_specs=out_specs,
        out_shape=out_shape,
        scratch_shapes=[pltpu.VMEM((2, seq, wb), F32), pltpu.VMEM((2, wb, wb), F32)],
        compiler_params=_cparams(),
        name="hgrn_lat" if s0 is not None else "hgrn_ctx",
    )(*args)


def _merge_kernel(h_ref, ua_ref, ub_ref, uc_ref, ud_ref, x_ref, mod_ref, wg_ref, wa_ref, wb_ref, wc_ref, wd_ref,
                  ps_ref, wo_ref, g2_ref, x1_ref, h2_ref):
    d = D_MODEL
    mod = mod_ref[0]
    g_m, sh_f, sc_f = mod[:, 2 * d:3 * d], mod[:, 3 * d:4 * d], mod[:, 4 * d:5 * d]
    for r0 in range(0, x_ref.shape[0], MERGE_SUB):
        rows = slice(r0, r0 + MERGE_SUB)
        h = h_ref[rows, :]
        merged = None
        for i, (u_ref, w_ref) in enumerate(((ua_ref, wa_ref), (ub_ref, wb_ref), (uc_ref, wc_ref), (ud_ref, wd_ref))):
            y = _dot(u_ref[rows, :], w_ref[0])
            if i == 3:
                y = y * ps_ref[0]
            term = _sigmoid(_dot(h, wg_ref[:, i * d:(i + 1) * d])) * y
            merged = term if merged is None else merged + term
        mix = _dot(merged.astype(BF16), wo_ref[0])
        x1 = x_ref[rows, :] + g_m * mix
        x1_ref[rows, :] = x1
        h2_ref[rows, :] = (_rmsnorm(x1, g2_ref[0]) * (1.0 + sc_f) + sh_f).astype(BF16)


def _merge(h, us, x, mod, mod_row, w_gate, w, layer, tokens_per_mod):
    t, d = x.shape
    tm = MERGE_TM
    tok = lambda width: pl.BlockSpec((tm, width), lambda i: (i, 0))
    weights = [w["w_out_a"], w["w_out_b"], w["w_out_c"], w["w_pool"], w["pool_scale"], w["w_o"], w["norm_ffn_g"]]
    return pl.pallas_call(
        _merge_kernel,
        grid=(t // tm,),
        in_specs=[tok(d)] + [tok(W_BR)] * 4 + [tok(d), _mod_spec(tm, tokens_per_mod, mod_row),
                                               pl.BlockSpec(w_gate.shape, lambda i: (0, 0))]
                 + [_layer_spec(a, layer) for a in weights],
        out_specs=[tok(d), tok(d)],
        out_shape=[jax.ShapeDtypeStruct((t, d), F32), jax.ShapeDtypeStruct((t, d), BF16)],
        compiler_params=_cparams(),
        name="merge",
    )(h, *us, x, mod, w_gate, *weights)


def _ffn_kernel(h2_ref, x1_ref, mod_ref, w13_ref, w2_ref, gf_ref, o_ref, *, final):
    d = D_MODEL
    w13_ref, w2_ref = w13_ref.at[0], w2_ref.at[0]
    g_f = mod_ref[0][:, 5 * d:6 * d]
    for r0 in range(0, x1_ref.shape[0], FFN_SUB):
        rows = slice(r0, r0 + FFN_SUB)
        h2 = h2_ref[rows, :]
        acc = None
        for lo in range(0, D_FF, FFN_CHUNK):
            hi = min(lo + FFN_CHUNK, D_FF)
            gate = _dot(h2, w13_ref[:, lo:hi])
            up = _dot(h2, w13_ref[:, D_FF + lo:D_FF + hi])
            part = _dot((_silu(gate) * up).astype(BF16), w2_ref[lo:hi, :])
            acc = part if acc is None else acc + part
        x2 = x1_ref[rows, :] + g_f * acc
        if final:
            x2 = _rmsnorm(x2, gf_ref[...])
        o_ref[rows, :] = x2


def _ffn(h2, x1, mod, mod_row, w13, w2, final_g, layer, tokens_per_mod, final):
    t, d = x1.shape
    tm = FFN_TM
    return pl.pallas_call(
        functools.partial(_ffn_kernel, final=final),
        grid=(t // tm,),
        in_specs=[pl.BlockSpec((tm, d), lambda i: (i, 0)), pl.BlockSpec((tm, d), lambda i: (i, 0)),
                  _mod_spec(tm, tokens_per_mod, mod_row), _layer_spec(w13, layer), _layer_spec(w2, layer),
                  pl.BlockSpec(final_g.shape, lambda i: (0, 0))],
        out_specs=pl.BlockSpec((tm, d), lambda i: (i, 0)),
        out_shape=jax.ShapeDtypeStruct((t, d), F32),
        compiler_params=_cparams(),
        name="ffn",
    )(h2, x1, mod, w13, w2, final_g)


def _block_diag_heads(s):
    eye = jnp.eye(H_B, dtype=s.dtype)
    bd = jnp.einsum("...hab,hg->...hagb", s, eye)
    return bd.reshape(s.shape[:-3] + (H_B * s.shape[-2], H_B * s.shape[-1]))


def kernel(x_prompt, x_sample, state_hgrn, c, c_ctx, ada_w, ada_b, norm_mix_g, w_in, conv_a_w, conv_a_b, ln_a_g,
           ln_a_b, w_out_a, hgrn_lb_logits, hgrn_norm_g, w_out_b, conv_c_w, w_out_c, pool_w, pool_scale, w_o,
           norm_ffn_g, ffn_w13, ffn_w2, final_norm_g):
    depth = w_in.shape[0]
    d = D_MODEL
    n_ctx, n_lat = x_prompt.shape[0], x_sample.shape[0]
    t_ctx, t_lat = n_ctx * CTX_SEQ, n_lat * LAT_SEQ

    n_rows = -(-(1 + n_lat) // 8) * 8
    c_rows = jnp.zeros((n_rows, d), F32).at[0].set(c_ctx).at[1:1 + n_lat].set(c)
    mod = _modulation(c_rows, ada_w, ada_b).reshape(depth * n_rows, 1, 6 * d)

    lb_all = jnp.cumsum(jax.nn.softmax(hgrn_lb_logits.astype(F32), axis=0), axis=0)
    lb_all = (lb_all - lb_all[:1]).reshape(depth, 1, 2 * W_BR)

    head_mask = _block_diag_heads(jnp.ones((H_B, HGRN_HD, HGRN_HD), F32))
    level_masks = jnp.asarray(_level_masks())

    rows = lambda a: a.reshape(depth, 1, -1)
    w = {
        "w_out_a": w_out_a.astype(BF16), "w_out_b": w_out_b.astype(BF16), "w_out_c": w_out_c.astype(BF16),
        "w_pool": _block_diag_heads(pool_w).astype(BF16), "pool_scale": rows(pool_scale),
        "w_o": w_o.astype(BF16), "norm_ffn_g": rows(norm_ffn_g),
        "conv_a_w": conv_a_w, "conv_a_b": rows(conv_a_b), "ln_a_g": rows(ln_a_g), "ln_a_b": rows(ln_a_b),
        "conv_c_w": conv_c_w,
    }
    norm_mix, hgrn_g = rows(norm_mix_g), rows(hgrn_norm_g)
    w13 = ffn_w13.astype(BF16)
    w2 = ffn_w2.astype(BF16)
    final_g = final_norm_g.reshape(1, d)

    y_p = x_prompt.reshape(t_ctx, d)
    y_s = x_sample.reshape(t_lat, d)
    ctx_states = []
    for l in range(depth):
        w_mix = w_in[l, :, :N_MIX].astype(BF16)
        w_gate = w_in[l, :, N_MIX:].astype(BF16)
        final = l == depth - 1
        outs = []
        for latent, x, tokens_per_mod in ((False, y_p, t_ctx), (True, y_s, LAT_SEQ)):
            mod_row = l * n_rows + (1 if latent else 0)
            h, z = _inproj(x, mod, mod_row, norm_mix, w_mix, lb_all, l, tokens_per_mod)
            u_a, u_c, u_d = _mixers(z, w, l, latent)
            hg = _hgrn(z, hgrn_g, head_mask, level_masks, state_hgrn if latent else None, l,
                       LAT_SEQ if latent else CTX_SEQ)
            if not latent:
                ctx_states.append(hg[1])
            x1, h2 = _merge(h, (u_a, hg[0], u_c, u_d), x, mod, mod_row, w_gate, w, l, tokens_per_mod)
            outs.append(_ffn(h2, x1, mod, mod_row, w13, w2, final_g, l, tokens_per_mod, final))
        y_p, y_s = outs

    new_state = jnp.stack(ctx_states, axis=1).astype(x_prompt.dtype)
    return (y_p.reshape(x_prompt.shape), y_s.reshape(x_sample.shape), new_state)
```

```python
import functools

import numpy as np
import jax
import jax.numpy as jnp
from jax import lax
from jax.experimental import pallas as pl
from jax.experimental.pallas import tpu as pltpu

F32 = jnp.float32
BF16 = jnp.bfloat16

D_MODEL = 1024
GRID_W = 64
EPS = 1e-6
W_BR = 256
CONV_A_K = 31
H_B = 4
HGRN_HD = 64
POOL_WINDOWS = (2, 4, 8, 16)
POOL_GC = 64
D_FF = 2816
N_MIX = 11 * W_BR
N_BRANCH = 4
N_Z = 9 * W_BR
CTX_SEQ = 256
LAT_SEQ = 2048

INPROJ_TM = 1024
MERGE_TM = 512
FFN_TM = 1024
FFN_CHUNK = 256
INPROJ_SUB = 128
MERGE_SUB = 256
FFN_SUB = 512

VMEM_LIMIT = 56 * 1024 * 1024
LOG2E = 1.4426950408889634


def _cparams(n_axes=1):
    return pltpu.CompilerParams(dimension_semantics=("arbitrary",) * n_axes, vmem_limit_bytes=VMEM_LIMIT)


def _sigmoid(x):
    return 1.0 / (1.0 + jnp.exp(-x))


def _silu(x):
    return x * _sigmoid(x)


def _rmsnorm(x, g):
    return x * lax.rsqrt(jnp.mean(x * x, axis=-1, keepdims=True) + EPS) * g


def _dot(a, b):
    return jnp.dot(a, b, preferred_element_type=F32)


def _layer_spec(a, layer):
    return pl.BlockSpec((1,) + a.shape[1:], lambda i: (layer,) + (0,) * (a.ndim - 1))


def _mod_spec(tm, tokens_per_mod, first_row):
    return pl.BlockSpec((1, 1, 6 * D_MODEL), lambda i: (first_row + (i * tm) // tokens_per_mod, 0, 0))


def _mod_kernel(c_ref, w_ref, b_ref, o_ref):
    s = _silu(c_ref[...]).astype(BF16)
    o_ref[0] = _dot(s, w_ref[0].astype(BF16)) + b_ref[0]


def _modulation(c_rows, ada_w, ada_b):
    depth, d, n = ada_w.shape
    nr = c_rows.shape[0]
    tn = 1536
    return pl.pallas_call(
        _mod_kernel,
        grid=(depth, n // tn),
        in_specs=[
            pl.BlockSpec((nr, d), lambda l, j: (0, 0)),
            pl.BlockSpec((1, d, tn), lambda l, j: (l, 0, j)),
            pl.BlockSpec((1, 1, tn), lambda l, j: (l, 0, j)),
        ],
        out_specs=pl.BlockSpec((1, nr, tn), lambda l, j: (l, 0, j)),
        out_shape=jax.ShapeDtypeStruct((depth, nr, n), F32),
        compiler_params=_cparams(2),
        name="modulation",
    )(c_rows, ada_w, ada_b.reshape(depth, 1, n))


def _log2_forget(z, lb):
    zs = z * LOG2E
    ls = jnp.minimum(zs, 0.0) - jnp.log2(1.0 + jnp.exp2(-jnp.abs(zs)))
    a = jnp.log2(lb)
    b = jnp.log1p(-lb) * LOG2E + ls
    return jnp.maximum(a, b) + jnp.log2(1.0 + jnp.exp2(-jnp.abs(a - b)))


def _inproj_kernel(x_ref, mod_ref, g_ref, w_ref, lb_ref, h_ref, z_ref):
    d, w = D_MODEL, W_BR
    mod = mod_ref[0]
    sh, sc = mod[:, 0:d], mod[:, d:2 * d]
    lb = lb_ref[0]
    for r0 in range(0, x_ref.shape[0], INPROJ_SUB):
        rows = slice(r0, r0 + INPROJ_SUB)
        h = (_rmsnorm(x_ref[rows, :], g_ref[0]) * (1.0 + sc) + sh).astype(BF16)
        h_ref[rows, :] = h

        def proj(j, n, h=h):
            return _dot(h, w_ref[:, j * w:(j + n) * w])

        za = proj(0, 2)
        z_ref[rows, 0 * w:1 * w] = za[:, 0:w] * _sigmoid(za[:, w:2 * w])
        zb = proj(2, 3)
        z_ref[rows, 1 * w:2 * w] = zb[:, 0:w]
        z_ref[rows, 2 * w:4 * w] = _log2_forget(zb[:, w:3 * w], lb)
        zv = proj(5, 2)
        z_ref[rows, 4 * w:5 * w] = zv[:, 0:w]
        z_ref[rows, 5 * w:6 * w] = _silu(zv[:, w:2 * w])
        zc = proj(7, 3)
        z_ref[rows, 6 * w:7 * w] = zc[:, 0:w]
        z_ref[rows, 7 * w:8 * w] = zc[:, w:2 * w] * zc[:, 2 * w:3 * w]
        z_ref[rows, 8 * w:9 * w] = proj(10, 1)


def _inproj(x, mod, mod_row, g, w_mix, lb, layer, tokens_per_mod):
    t, d = x.shape
    tm = INPROJ_TM
    return pl.pallas_call(
        _inproj_kernel,
        grid=(t // tm,),
        in_specs=[
            pl.BlockSpec((tm, d), lambda i: (i, 0)),
            _mod_spec(tm, tokens_per_mod, mod_row),
            _layer_spec(g, layer),
            pl.BlockSpec((pl.Squeezed(), d, N_MIX), lambda i: (layer, 0, 0)),
            _layer_spec(lb, layer),
        ],
        out_specs=[
            pl.BlockSpec((tm, d), lambda i: (i, 0)),
            pl.BlockSpec((tm, N_Z), lambda i: (i, 0)),
        ],
        out_shape=[jax.ShapeDtypeStruct((t, d), BF16), jax.ShapeDtypeStruct((t, N_Z), F32)],
        compiler_params=_cparams(),
        name="inproj",
    )(x, mod, g, w_mix, lb)


def _row_index(rows):
    return lax.broadcasted_iota(jnp.int32, (rows, W_BR), 0)


def _shift_rows(x, j, pos, seg):
    if j == 0:
        return x
    y = pltpu.roll(x, (-j) % x.shape[0], axis=0)
    valid = (pos < seg - j) if j > 0 else (pos >= -j)
    return jnp.where(valid, y, 0.0)


def _conv_a_ln(x, wa_ref, ba_ref, lg_ref, lbeta_ref):
    half = CONV_A_K // 2
    seg = x.shape[0]
    pad = 2 * SUBLANES
    zeros = jnp.zeros((pad, x.shape[1]), F32)
    xpad = jnp.concatenate([zeros, x, zeros], axis=0)
    n = seg + 2 * pad
    rotated = [xpad] + [pltpu.roll(xpad, n - r, axis=0) for r in range(1, SUBLANES)]
    acc = None
    for j in range(-half, half + 1):
        whole, r = divmod(pad + j, SUBLANES)
        term = rotated[r][whole * SUBLANES:whole * SUBLANES + seg] * wa_ref[half + j:half + j + 1, :]
        acc = term if acc is None else acc + term
    u = acc + ba_ref[...]
    mu = jnp.mean(u, axis=-1, keepdims=True)
    var = jnp.mean(jnp.square(u - mu), axis=-1, keepdims=True)
    return _silu((u - mu) * lax.rsqrt(var + EPS) * lg_ref[...] + lbeta_ref[...])


LANE_TILE = 128
SUBLANES = 8


def _pool_lane_tiles():
    per_tile = LANE_TILE // POOL_GC
    tiles = []
    for i in range(W_BR // LANE_TILE):
        lane = lax.broadcasted_iota(jnp.int32, (1, LANE_TILE), 1) + i * LANE_TILE
        half = jnp.left_shift(1, lane // POOL_GC)
        reach = max(POOL_WINDOWS[i * per_tile:(i + 1) * per_tile]) // 2
        tiles.append((slice(i * LANE_TILE, (i + 1) * LANE_TILE), half, reach))
    return tiles


def _mix_ctx_kernel(ua_ref, cb_ref, cc_ref, d_ref, wa_ref, ba_ref, lg_ref, lbeta_ref, wc_ref,
                    oa_ref, oc_ref, od_ref):
    wa_ref, ba_ref, lg_ref, lbeta_ref, wc_ref = (r.at[0] for r in (wa_ref, ba_ref, lg_ref, lbeta_ref, wc_ref))
    seg = CTX_SEQ
    pos = _row_index(seg)
    posl = lax.broadcasted_iota(jnp.int32, (seg, LANE_TILE), 0)
    tiles = _pool_lane_tiles()

    def body(s, carry):
        rows = pl.ds(pl.multiple_of(s * seg, seg), seg)
        oa_ref[rows, :] = _conv_a_ln(ua_ref[rows, :], wa_ref, ba_ref, lg_ref, lbeta_ref).astype(BF16)
        cc = cc_ref[rows, :]
        u = (_shift_rows(cc, -1, pos, seg) * wc_ref[0:1, :] + cc * wc_ref[1:2, :]
             + _shift_rows(cc, 1, pos, seg) * wc_ref[2:3, :])
        oc_ref[rows, :] = (cb_ref[rows, :] * u).astype(BF16)
        for lanes, half, reach in tiles:
            dd = d_ref[rows, lanes]
            acc = jnp.zeros_like(dd)
            cnt = jnp.zeros_like(dd)
            for j in range(-reach, reach):
                lane_on = (j >= -half) & (j <= half - 1)
                valid = (posl + j >= 0) & (posl + j < seg) & lane_on
                acc = acc + jnp.where(valid, _shift_rows(dd, j, posl, seg), 0.0)
                cnt = cnt + jnp.where(valid, 1.0, 0.0)
            od_ref[rows, lanes] = (acc / cnt - dd).astype(BF16)
        return carry

    lax.fori_loop(0, ua_ref.shape[0] // seg, body, 0)


def _mix_lat_kernel(ua_ref, cb_ref, cc_ref, d_ref, wa_ref, ba_ref, lg_ref, lbeta_ref, wc_ref,
                    oa_ref, oc_ref, od_ref):
    wa_ref, ba_ref, lg_ref, lbeta_ref, wc_ref = (r.at[0] for r in (wa_ref, ba_ref, lg_ref, lbeta_ref, wc_ref))
    seg = GRID_W
    n_rows = LAT_SEQ // GRID_W
    pos = _row_index(seg)
    tiles = _pool_lane_tiles()

    def grid_row(ref, r, j, lanes=slice(None)):
        rr = jnp.clip(r + j, 0, n_rows - 1)
        ok = jnp.where(jnp.logical_and(r + j >= 0, r + j < n_rows), 1.0, 0.0)
        return ref[pl.ds(pl.multiple_of(rr * seg, seg), seg), lanes], ok

    def body(r, carry):
        rows = pl.ds(pl.multiple_of(r * seg, seg), seg)
        oa_ref[rows, :] = _conv_a_ln(ua_ref[rows, :], wa_ref, ba_ref, lg_ref, lbeta_ref).astype(BF16)
        up, ok_up = grid_row(cc_ref, r, -1)
        dn, ok_dn = grid_row(cc_ref, r, 1)
        u = up * (wc_ref[0:1, :] * ok_up) + cc_ref[rows, :] * wc_ref[1:2, :] + dn * (wc_ref[2:3, :] * ok_dn)
        oc_ref[rows, :] = (cb_ref[rows, :] * u).astype(BF16)
        for lanes, half, reach in tiles:
            acc = jnp.zeros((seg, LANE_TILE), F32)
            cnt = jnp.zeros((1, LANE_TILE), F32)
            for j in range(-reach, reach):
                lane_on = jnp.where((j >= -half) & (j <= half - 1), 1.0, 0.0)
                x, ok = grid_row(d_ref, r, j, lanes)
                wgt = lane_on * ok
                acc = acc + x * wgt
                cnt = cnt + wgt
            od_ref[rows, lanes] = (acc / cnt - d_ref[rows, lanes]).astype(BF16)
        return carry

    lax.fori_loop(0, n_rows, body, 0)


def _mixers(z, p, layer, latent):
    t = z.shape[0]
    blk = LAT_SEQ
    zspec = lambda col: pl.BlockSpec((blk, W_BR), lambda i, col=col: (i, col))
    params = [p["conv_a_w"], p["conv_a_b"], p["ln_a_g"], p["ln_a_b"], p["conv_c_w"]]
    out = jax.ShapeDtypeStruct((t, W_BR), BF16)
    return pl.pallas_call(
        _mix_lat_kernel if latent else _mix_ctx_kernel,
        grid=(t // blk,),
        in_specs=[zspec(0), zspec(6), zspec(7), zspec(8)] + [_layer_spec(a, layer) for a in params],
        out_specs=[pl.BlockSpec((blk, W_BR), lambda i: (i, 0))] * 3,
        out_shape=[out, out, out],
        compiler_params=_cparams(),
        name="mix_lat" if latent else "mix_ctx",
    )(z, z, z, z, *params)


SB = 256
SUB = 8
LEVELS = (8, 16, 32, 64, 128)
NQ = SB // 2
CTX_PER_STEP = 2


def _level_spans(m, upper):
    off = m if upper else 0
    return [(i * 2 * m + off, i * 2 * m + off + m) for i in range(SB // (2 * m))]


def _take(x, spans):
    parts = [x[a:b] for a, b in spans]
    return parts[0] if len(parts) == 1 else jnp.concatenate(parts, axis=0)


def _level_masks():
    tq = np.arange(H_B * NQ)[:, None] % NQ
    ts = np.arange(NQ)[None, :]
    return np.stack([((tq // m) == (ts // m)) for m in LEVELS[:-1]]).astype(np.float32)


def _cumsum_rows(x):
    g = SB // SUB
    x3 = x.reshape(g, SUB, x.shape[-1])
    sub = lax.broadcasted_iota(jnp.int32, x3.shape, 1)
    step = 1
    while step < SUB:
        x3 = x3 + jnp.where(sub >= step, pltpu.roll(x3, step, axis=1), 0.0)
        step *= 2
    tot = x3[:, SUB - 1:SUB, :]
    offs = [jnp.zeros_like(tot[0:1])]
    for i in range(1, g):
        offs.append(offs[-1] + tot[i - 1:i])
    return (x3 + jnp.concatenate(offs, axis=0)).reshape(x.shape)


def _hgrn_kernel(*refs, has_s0):
    if has_s0:
        q_ref, lff_ref, lfb_ref, v_ref, sg_ref, ng_ref, hm_ref, lm_ref, s0_ref, o_ref, oacc, st = refs
        sfin_ref = None
    else:
        q_ref, lff_ref, lfb_ref, v_ref, sg_ref, ng_ref, hm_ref, lm_ref, o_ref, sfin_ref, oacc, st = refs
        s0_ref = None
    seq, w = q_ref.shape
    n_sb = seq // SB
    with_inter = has_s0
    head_ones = hm_ref[...].astype(BF16)
    lane_head = lax.broadcasted_iota(jnp.int32, (1, w), 1) // HGRN_HD
    in_head = [lane_head == h for h in range(H_B)]
    sub3 = lax.broadcasted_iota(jnp.int32, (SB // SUB, SUB, w), 1)
    nt = (((1,), (1,)), ((), ()))
    tn = (((0,), (0,)), ((), ()))
    head_blocks = [slice(h * HGRN_HD, (h + 1) * HGRN_HD) for h in range(H_B)]

    for direction in range(2 if with_inter else 0):
        st[direction] = jnp.zeros((w, w), F32)
        if s0_ref is not None:
            for h, blk in enumerate(head_blocks):
                st[direction, blk, blk] = s0_ref[0, 0, direction, h]
            st[direction] = st[direction].T

    def sb_body(si, carry):
        for direction, lf_ref in ((0, lff_ref), (1, lfb_ref)):
            back = direction == 1
            base = pl.multiple_of(((n_sb - 1 - si) if back else si) * SB, SB)
            rows = pl.ds(base, SB)
            lf = lf_ref[rows, :]
            q = q_ref[rows, :]
            v = v_ref[rows, :]
            vb = v.astype(BF16)
            key = 1.0 - jnp.exp2(lf)
            g = _cumsum_rows(lf)
            tot = g[SB - 1:SB, :]
            b = (lf - g) if back else g

            q3, k3, b3, v3 = (x.reshape(SB // SUB, SUB, w) for x in (q, key, b, v))
            acc = _dot((q * key).astype(BF16), head_ones) * v
            for dist in range(1, SUB):
                sft = (SUB - dist) if back else dist
                valid = (sub3 < SUB - dist) if back else (sub3 >= dist)
                e = jnp.where(valid, q3 * pltpu.roll(k3, sft, axis=1) * jnp.exp2(b3 - pltpu.roll(b3, sft, axis=1)), 0.0)
                red = _dot(e.reshape(SB, w).astype(BF16), head_ones)
                acc = acc + red * pltpu.roll(v3, sft, axis=1).reshape(SB, w)
            oacc[direction, rows, :] = acc

            for li, m in enumerate(LEVELS):
                lower, upper = _level_spans(m, False), _level_spans(m, True)
                qs, ks = (lower, upper) if back else (upper, lower)
                edge = [(a + m) if back else (a + m - 1) for a, _ in lower]
                bm = jnp.concatenate([jnp.broadcast_to(b[r:r + 1], (m, w)) for r in edge], axis=0)
                qg = _take(q, qs) * jnp.exp2(_take(b, qs) - bm)
                kg = (_take(key, ks) * jnp.exp2(bm - _take(b, ks))).astype(BF16)
                stacked = jnp.concatenate([jnp.where(in_head[h], qg, 0.0) for h in range(H_B)], axis=0)
                p = lax.dot_general(stacked.astype(BF16), kg, nt, preferred_element_type=F32)
                if m < NQ:
                    p = p * lm_ref[li]
                r = _dot(p.astype(BF16), _take(vb, ks))
                res = jnp.where(in_head[0], r[0:NQ], 0.0)
                for h in range(1, H_B):
                    res = res + jnp.where(in_head[h], r[h * NQ:(h + 1) * NQ], 0.0)
                for bi, (a0, _) in enumerate(qs):
                    dst = pl.ds(base + a0, m)
                    oacc[direction, dst, :] = oacc[direction, dst, :] + res[bi * m:(bi + 1) * m]

            cq, ck = (tot, 0.0) if back else (0.0, tot)
            kp = (key * jnp.exp2(ck - b)).astype(BF16)
            if with_inter:
                s_old = st[direction]
                qp = (q * jnp.exp2(b + cq)).astype(BF16)
                oacc[direction, rows, :] = oacc[direction, rows, :] + lax.dot_general(
                    qp, s_old.astype(BF16), nt, preferred_element_type=F32)
                kv_t = lax.dot_general(vb, kp, tn, preferred_element_type=F32)
                st[direction] = s_old * jnp.exp2(tot) + kv_t * hm_ref[...]
            else:
                kv = lax.dot_general(kp, vb, tn, preferred_element_type=F32)
                for h, blk in enumerate(head_blocks):
                    sfin_ref[(n_sb - 1 - si) if back else si, direction, h] = kv[blk, blk]
        return carry

    lax.fori_loop(0, n_sb, sb_body, 0, unroll=2 if n_sb % 2 == 0 else 1)

    def out_body(i, carry):
        rows = pl.ds(pl.multiple_of(i * SB, SB), SB)
        o = oacc[0, rows, :] + oacc[1, rows, :]
        o2 = o * o
        hi = o2.astype(BF16)
        lo = (o2 - hi.astype(F32)).astype(BF16)
        ms = (_dot(hi, head_ones) + _dot(lo, head_ones)) * (1.0 / HGRN_HD)
        o_ref[rows, :] = (o * lax.rsqrt(ms + EPS) * ng_ref[0] * sg_ref[rows, :]).astype(BF16)
        return carry

    lax.fori_loop(0, n_sb, out_body, 0)


def _hgrn(z, norm_g, head_mask, level_masks, s0, layer):
    t = z.shape[0]
    if s0 is None:
        assert CTX_SEQ == SB
        n_seq, seq = t // CTX_SEQ, CTX_SEQ * CTX_PER_STEP
        state_block = (CTX_PER_STEP, 2, H_B, HGRN_HD, HGRN_HD)
    else:
        n_seq, seq = t // LAT_SEQ, LAT_SEQ
    zspec = lambda col: pl.BlockSpec((seq, W_BR), lambda i, col=col: (i, col))
    wb = H_B * HGRN_HD
    in_specs = [zspec(1), zspec(2), zspec(3), zspec(4), zspec(5),
                _layer_spec(norm_g, layer),
                pl.BlockSpec((wb, wb), lambda i: (0, 0)),
                pl.BlockSpec(level_masks.shape, lambda i: (0, 0, 0))]
    args = [z, z, z, z, z, norm_g, head_mask, level_masks]
    out_specs = [pl.BlockSpec((seq, wb), lambda i: (i, 0))]
    out_shape = [jax.ShapeDtypeStruct((t, wb), BF16)]
    if s0 is not None:
        in_specs.append(pl.BlockSpec((1, 1, 2, H_B, HGRN_HD, HGRN_HD), lambda i: (i, layer, 0, 0, 0, 0)))
        args.append(s0)
    else:
        out_specs.append(pl.BlockSpec(state_block, lambda i: (i, 0, 0, 0, 0)))
        out_shape.append(jax.ShapeDtypeStruct((n_seq, 2, H_B, HGRN_HD, HGRN_HD), F32))
    return pl.pallas_call(
        functools.partial(_hgrn_kernel, has_s0=s0 is not None),
        grid=(t // seq,),
        in_specs=in_specs,
        out_specs=out_specs,
        out_shape=out_shape,
        scratch_shapes=[pltpu.VMEM((2, seq, wb), F32), pltpu.VMEM((2, wb, wb), F32)],
        compiler_params=_cparams(),
        name="hgrn_lat" if s0 is not None else "hgrn_ctx",
    )(*args)


def _merge_kernel(h_ref, ua_ref, ub_ref, uc_ref, ud_ref, x_ref, mod_ref, wg_ref, wa_ref, wb_ref, wc_ref, wd_ref,
                  ps_ref, wo_ref, g2_ref, x1_ref, h2_ref):
    d = D_MODEL
    mod = mod_ref[0]
    g_m, sh_f, sc_f = mod[:, 2 * d:3 * d], mod[:, 3 * d:4 * d], mod[:, 4 * d:5 * d]
    for r0 in range(0, x_ref.shape[0], MERGE_SUB):
        rows = slice(r0, r0 + MERGE_SUB)
        h = h_ref[rows, :]
        merged = None
        for i, (u_ref, w_ref) in enumerate(((ua_ref, wa_ref), (ub_ref, wb_ref), (uc_ref, wc_ref), (ud_ref, wd_ref))):
            y = _dot(u_ref[rows, :], w_ref[0])
            if i == 3:
                y = y * ps_ref[0]
            term = _sigmoid(_dot(h, wg_ref[0, :, i * d:(i + 1) * d])) * y
            merged = term if merged is None else merged + term
        mix = _dot(merged.astype(BF16), wo_ref[0])
        x1 = x_ref[rows, :] + g_m * mix
        x1_ref[rows, :] = x1
        h2_ref[rows, :] = (_rmsnorm(x1, g2_ref[0]) * (1.0 + sc_f) + sh_f).astype(BF16)


def _merge(h, us, x, mod, mod_row, w_gate, w, layer, tokens_per_mod):
    t, d = x.shape
    tm = MERGE_TM
    tok = lambda width: pl.BlockSpec((tm, width), lambda i: (i, 0))
    weights = [w["w_out_a"], w["w_out_b"], w["w_out_c"], w["w_pool"], w["pool_scale"], w["w_o"], w["norm_ffn_g"]]
    return pl.pallas_call(
        _merge_kernel,
        grid=(t // tm,),
        in_specs=[tok(d)] + [tok(W_BR)] * 4 + [tok(d), _mod_spec(tm, tokens_per_mod, mod_row),
                                               pl.BlockSpec((pl.Element(1), pl.Element(d), pl.Element(N_BRANCH * d)),
                                                            lambda i: (layer, 0, N_MIX))]
                 + [_layer_spec(a, layer) for a in weights],
        out_specs=[tok(d), tok(d)],
        out_shape=[jax.ShapeDtypeStruct((t, d), F32), jax.ShapeDtypeStruct((t, d), BF16)],
        compiler_params=_cparams(),
        name="merge",
    )(h, *us, x, mod, w_gate, *weights)


def _ffn_kernel(h2_ref, x1_ref, mod_ref, w13_ref, w2_ref, gf_ref, o_ref, *, final):
    d = D_MODEL
    w13_ref, w2_ref = w13_ref.at[0], w2_ref.at[0]
    g_f = mod_ref[0][:, 5 * d:6 * d]
    for r0 in range(0, x1_ref.shape[0], FFN_SUB):
        rows = slice(r0, r0 + FFN_SUB)
        h2 = h2_ref[rows, :]
        acc = None
        for lo in range(0, D_FF, FFN_CHUNK):
            hi = min(lo + FFN_CHUNK, D_FF)
            gate = _dot(h2, w13_ref[:, lo:hi])
            up = _dot(h2, w13_ref[:, D_FF + lo:D_FF + hi])
            part = _dot((_silu(gate) * up).astype(BF16), w2_ref[lo:hi, :])
            acc = part if acc is None else acc + part
        x2 = x1_ref[rows, :] + g_f * acc
        if final:
            x2 = _rmsnorm(x2, gf_ref[...])
        o_ref[rows, :] = x2


def _ffn(h2, x1, mod, mod_row, w13, w2, final_g, layer, tokens_per_mod, final):
    t, d = x1.shape
    tm = FFN_TM
    return pl.pallas_call(
        functools.partial(_ffn_kernel, final=final),
        grid=(t // tm,),
        in_specs=[pl.BlockSpec((tm, d), lambda i: (i, 0)), pl.BlockSpec((tm, d), lambda i: (i, 0)),
                  _mod_spec(tm, tokens_per_mod, mod_row), _layer_spec(w13, layer), _layer_spec(w2, layer),
                  pl.BlockSpec(final_g.shape, lambda i: (0, 0))],
        out_specs=pl.BlockSpec((tm, d), lambda i: (i, 0)),
        out_shape=jax.ShapeDtypeStruct((t, d), F32),
        compiler_params=_cparams(),
        name="ffn",
    )(h2, x1, mod, w13, w2, final_g)


def _block_diag_heads(s):
    eye = jnp.eye(H_B, dtype=s.dtype)
    bd = jnp.einsum("...hab,hg->...hagb", s, eye)
    return bd.reshape(s.shape[:-3] + (H_B * s.shape[-2], H_B * s.shape[-1]))


def kernel(x_prompt, x_sample, state_hgrn, c, c_ctx, ada_w, ada_b, norm_mix_g, w_in, conv_a_w, conv_a_b, ln_a_g,
           ln_a_b, w_out_a, hgrn_lb_logits, hgrn_norm_g, w_out_b, conv_c_w, w_out_c, pool_w, pool_scale, w_o,
           norm_ffn_g, ffn_w13, ffn_w2, final_norm_g):
    depth = w_in.shape[0]
    d = D_MODEL
    n_ctx, n_lat = x_prompt.shape[0], x_sample.shape[0]
    t_ctx, t_lat = n_ctx * CTX_SEQ, n_lat * LAT_SEQ

    n_rows = -(-(1 + n_lat) // 8) * 8
    c_rows = jnp.zeros((n_rows, d), F32).at[0].set(c_ctx).at[1:1 + n_lat].set(c)
    mod = _modulation(c_rows, ada_w, ada_b).reshape(depth * n_rows, 1, 6 * d)

    lb_all = jnp.cumsum(jax.nn.softmax(hgrn_lb_logits.astype(F32), axis=0), axis=0)
    lb_all = (lb_all - lb_all[:1]).reshape(depth, 1, 2 * W_BR)

    head_mask = _block_diag_heads(jnp.ones((H_B, HGRN_HD, HGRN_HD), F32))
    level_masks = jnp.asarray(_level_masks())

    rows = lambda a: a.reshape(depth, 1, -1)
    w = {
        "w_out_a": w_out_a.astype(BF16), "w_out_b": w_out_b.astype(BF16), "w_out_c": w_out_c.astype(BF16),
        "w_pool": _block_diag_heads(pool_w).astype(BF16), "pool_scale": rows(pool_scale),
        "w_o": w_o.astype(BF16), "norm_ffn_g": rows(norm_ffn_g),
        "conv_a_w": conv_a_w, "conv_a_b": rows(conv_a_b), "ln_a_g": rows(ln_a_g), "ln_a_b": rows(ln_a_b),
        "conv_c_w": conv_c_w,
    }
    norm_mix, hgrn_g = rows(norm_mix_g), rows(hgrn_norm_g)
    w13 = ffn_w13.astype(BF16)
    w2 = ffn_w2.astype(BF16)
    final_g = final_norm_g.reshape(1, d)

    y_p = x_prompt.reshape(t_ctx, d)
    y_s = x_sample.reshape(t_lat, d)
    ctx_states = []
    w_mix = w_gate = w_in.astype(BF16)
    for l in range(depth):
        final = l == depth - 1
        outs = []
        for latent, x, tokens_per_mod in ((False, y_p, t_ctx), (True, y_s, LAT_SEQ)):
            mod_row = l * n_rows + (1 if latent else 0)
            h, z = _inproj(x, mod, mod_row, norm_mix, w_mix, lb_all, l, tokens_per_mod)
            u_a, u_c, u_d = _mixers(z, w, l, latent)
            hg = _hgrn(z, hgrn_g, head_mask, level_masks, state_hgrn if latent else None, l)
            if not latent:
                ctx_states.append(hg[1])
            x1, h2 = _merge(h, (u_a, hg[0], u_c, u_d), x, mod, mod_row, w_gate, w, l, tokens_per_mod)
            outs.append(_ffn(h2, x1, mod, mod_row, w13, w2, final_g, l, tokens_per_mod, final))
        y_p, y_s = outs

    new_state = jnp.stack(ctx_states, axis=1).astype(x_prompt.dtype)
    return (y_p.reshape(x_prompt.shape), y_s.reshape(x_sample.shape), new_state)
```

```python
import functools

import numpy as np
import jax
import jax.numpy as jnp
from jax import lax
from jax.experimental import pallas as pl
from jax.experimental.pallas import tpu as pltpu

F32 = jnp.float32
BF16 = jnp.bfloat16

D_MODEL = 1024
GRID_W = 64
EPS = 1e-6
W_BR = 256
CONV_A_K = 31
H_B = 4
HGRN_HD = 64
POOL_WINDOWS = (2, 4, 8, 16)
POOL_GC = 64
D_FF = 2816
N_MIX = 11 * W_BR
N_BRANCH = 4
N_Z = 9 * W_BR
CTX_SEQ = 256
LAT_SEQ = 2048

INPROJ_TM = 1024
MERGE_TM = 512
FFN_TM = 1024
FFN_CHUNK = 256
INPROJ_SUB = 128
MERGE_SUB = 256
FFN_SUB = 512

VMEM_LIMIT = 56 * 1024 * 1024
LOG2E = 1.4426950408889634


def _cparams(n_axes=1):
    return pltpu.CompilerParams(dimension_semantics=("arbitrary",) * n_axes, vmem_limit_bytes=VMEM_LIMIT)


def _sigmoid(x):
    return 1.0 / (1.0 + jnp.exp(-x))


def _silu(x):
    return x * _sigmoid(x)


def _rmsnorm(x, g):
    return x * lax.rsqrt(jnp.mean(x * x, axis=-1, keepdims=True) + EPS) * g


def _dot(a, b):
    return jnp.dot(a, b, preferred_element_type=F32)


def _layer_spec(a, layer):
    return pl.BlockSpec((1,) + a.shape[1:], lambda i: (layer,) + (0,) * (a.ndim - 1))


def _mod_spec(tm, tokens_per_mod, first_row):
    return pl.BlockSpec((1, 1, 6 * D_MODEL), lambda i: (first_row + (i * tm) // tokens_per_mod, 0, 0))


def _mod_kernel(c_ref, w_ref, b_ref, o_ref):
    s = _silu(c_ref[...]).astype(BF16)
    o_ref[0] = _dot(s, w_ref[0].astype(BF16)) + b_ref[0]


def _modulation(c_rows, ada_w, ada_b):
    depth, d, n = ada_w.shape
    nr = c_rows.shape[0]
    tn = 1536
    return pl.pallas_call(
        _mod_kernel,
        grid=(depth, n // tn),
        in_specs=[
            pl.BlockSpec((nr, d), lambda l, j: (0, 0)),
            pl.BlockSpec((1, d, tn), lambda l, j: (l, 0, j)),
            pl.BlockSpec((1, 1, tn), lambda l, j: (l, 0, j)),
        ],
        out_specs=pl.BlockSpec((1, nr, tn), lambda l, j: (l, 0, j)),
        out_shape=jax.ShapeDtypeStruct((depth, nr, n), F32),
        compiler_params=_cparams(2),
        name="modulation",
    )(c_rows, ada_w, ada_b.reshape(depth, 1, n))


def _log2_forget(z, lb):
    zs = z * LOG2E
    ls = jnp.minimum(zs, 0.0) - jnp.log2(1.0 + jnp.exp2(-jnp.abs(zs)))
    a = jnp.log2(lb)
    b = jnp.log1p(-lb) * LOG2E + ls
    return jnp.maximum(a, b) + jnp.log2(1.0 + jnp.exp2(-jnp.abs(a - b)))


def _inproj_kernel(x_ref, mod_ref, g_ref, w_ref, lb_ref, h_ref, z_ref):
    d, w = D_MODEL, W_BR
    mod = mod_ref[0]
    sh, sc = mod[:, 0:d], mod[:, d:2 * d]
    lb = lb_ref[0]
    for r0 in range(0, x_ref.shape[0], INPROJ_SUB):
        rows = slice(r0, r0 + INPROJ_SUB)
        h = (_rmsnorm(x_ref[rows, :], g_ref[0]) * (1.0 + sc) + sh).astype(BF16)
        h_ref[rows, :] = h

        def proj(j, n, h=h):
            return _dot(h, w_ref[:, j * w:(j + n) * w])

        za = proj(0, 2)
        z_ref[rows, 0 * w:1 * w] = za[:, 0:w] * _sigmoid(za[:, w:2 * w])
        zb = proj(2, 3)
        z_ref[rows, 1 * w:2 * w] = zb[:, 0:w]
        z_ref[rows, 2 * w:4 * w] = _log2_forget(zb[:, w:3 * w], lb)
        zv = proj(5, 2)
        z_ref[rows, 4 * w:5 * w] = zv[:, 0:w]
        z_ref[rows, 5 * w:6 * w] = _silu(zv[:, w:2 * w])
        zc = proj(7, 3)
        z_ref[rows, 6 * w:7 * w] = zc[:, 0:w]
        z_ref[rows, 7 * w:8 * w] = zc[:, w:2 * w] * zc[:, 2 * w:3 * w]
        z_ref[rows, 8 * w:9 * w] = proj(10, 1)


def _inproj(x, mod, mod_row, g, w_mix, lb, layer, tokens_per_mod):
    t, d = x.shape
    tm = INPROJ_TM
    return pl.pallas_call(
        _inproj_kernel,
        grid=(t // tm,),
        in_specs=[
            pl.BlockSpec((tm, d), lambda i: (i, 0)),
            _mod_spec(tm, tokens_per_mod, mod_row),
            _layer_spec(g, layer),
            pl.BlockSpec((pl.Squeezed(), d, N_MIX), lambda i: (layer, 0, 0)),
            _layer_spec(lb, layer),
        ],
        out_specs=[
            pl.BlockSpec((tm, d), lambda i: (i, 0)),
            pl.BlockSpec((tm, N_Z), lambda i: (i, 0)),
        ],
        out_shape=[jax.ShapeDtypeStruct((t, d), BF16), jax.ShapeDtypeStruct((t, N_Z), F32)],
        compiler_params=_cparams(),
        name="inproj",
    )(x, mod, g, w_mix, lb)


def _row_index(rows):
    return lax.broadcasted_iota(jnp.int32, (rows, W_BR), 0)


def _shift_rows(x, j, pos, seg):
    if j == 0:
        return x
    y = pltpu.roll(x, (-j) % x.shape[0], axis=0)
    valid = (pos < seg - j) if j > 0 else (pos >= -j)
    return jnp.where(valid, y, 0.0)


def _conv_a_ln(x, wa_ref, ba_ref, lg_ref, lbeta_ref):
    half = CONV_A_K // 2
    seg = x.shape[0]
    pad = 2 * SUBLANES
    zeros = jnp.zeros((pad, x.shape[1]), F32)
    xpad = jnp.concatenate([zeros, x, zeros], axis=0)
    n = seg + 2 * pad
    rotated = [xpad] + [pltpu.roll(xpad, n - r, axis=0) for r in range(1, SUBLANES)]
    acc = None
    for j in range(-half, half + 1):
        whole, r = divmod(pad + j, SUBLANES)
        term = rotated[r][whole * SUBLANES:whole * SUBLANES + seg] * wa_ref[half + j:half + j + 1, :]
        acc = term if acc is None else acc + term
    u = acc + ba_ref[...]
    mu = jnp.mean(u, axis=-1, keepdims=True)
    var = jnp.mean(jnp.square(u - mu), axis=-1, keepdims=True)
    return _silu((u - mu) * lax.rsqrt(var + EPS) * lg_ref[...] + lbeta_ref[...])


LANE_TILE = 128
SUBLANES = 8


def _pool_lane_tiles():
    per_tile = LANE_TILE // POOL_GC
    tiles = []
    for i in range(W_BR // LANE_TILE):
        lane = lax.broadcasted_iota(jnp.int32, (1, LANE_TILE), 1) + i * LANE_TILE
        half = jnp.left_shift(1, lane // POOL_GC)
        reach = max(POOL_WINDOWS[i * per_tile:(i + 1) * per_tile]) // 2
        tiles.append((slice(i * LANE_TILE, (i + 1) * LANE_TILE), half, reach))
    return tiles


SB = 256
SUB = 8
LEVELS = (8, 16, 32, 64, 128)
NQ = SB // 2
CTX_PER_STEP = 2


def _level_spans(m, upper):
    off = m if upper else 0
    return [(i * 2 * m + off, i * 2 * m + off + m) for i in range(SB // (2 * m))]


def _take(x, spans):
    parts = [x[a:b] for a, b in spans]
    return parts[0] if len(parts) == 1 else jnp.concatenate(parts, axis=0)


def _level_masks():
    tq = np.arange(H_B * NQ)[:, None] % NQ
    ts = np.arange(NQ)[None, :]
    return np.stack([((tq // m) == (ts // m)) for m in LEVELS[:-1]]).astype(np.float32)


def _cumsum_rows(x):
    g = SB // SUB
    x3 = x.reshape(g, SUB, x.shape[-1])
    sub = lax.broadcasted_iota(jnp.int32, x3.shape, 1)
    step = 1
    while step < SUB:
        x3 = x3 + jnp.where(sub >= step, pltpu.roll(x3, step, axis=1), 0.0)
        step *= 2
    tot = x3[:, SUB - 1:SUB, :]
    offs = [jnp.zeros_like(tot[0:1])]
    for i in range(1, g):
        offs.append(offs[-1] + tot[i - 1:i])
    return (x3 + jnp.concatenate(offs, axis=0)).reshape(x.shape)


def _hgrn_kernel(*refs, has_s0):
    if has_s0:
        q_ref, lff_ref, lfb_ref, v_ref, sg_ref, ng_ref, hm_ref, lm_ref, s0_ref, o_ref, oacc, st = refs
        sfin_ref = None
    else:
        q_ref, lff_ref, lfb_ref, v_ref, sg_ref, ng_ref, hm_ref, lm_ref, o_ref, sfin_ref, oacc, st = refs
        s0_ref = None
    seq, w = q_ref.shape
    n_sb = seq // SB
    with_inter = has_s0
    head_ones = hm_ref[...].astype(BF16)
    lane_head = lax.broadcasted_iota(jnp.int32, (1, w), 1) // HGRN_HD
    in_head = [lane_head == h for h in range(H_B)]
    sub3 = lax.broadcasted_iota(jnp.int32, (SB // SUB, SUB, w), 1)
    nt = (((1,), (1,)), ((), ()))
    tn = (((0,), (0,)), ((), ()))
    head_blocks = [slice(h * HGRN_HD, (h + 1) * HGRN_HD) for h in range(H_B)]

    for direction in range(2 if with_inter else 0):
        st[direction] = jnp.zeros((w, w), F32)
        if s0_ref is not None:
            for h, blk in enumerate(head_blocks):
                st[direction, blk, blk] = s0_ref[0, 0, direction, h]
            st[direction] = st[direction].T

    def sb_body(si, carry):
        for direction, lf_ref in ((0, lff_ref), (1, lfb_ref)):
            back = direction == 1
            base = pl.multiple_of(((n_sb - 1 - si) if back else si) * SB, SB)
            rows = pl.ds(base, SB)
            lf = lf_ref[rows, :]
            q = q_ref[rows, :]
            v = v_ref[rows, :]
            vb = v.astype(BF16)
            key = 1.0 - jnp.exp2(lf)
            g = _cumsum_rows(lf)
            tot = g[SB - 1:SB, :]
            b = (lf - g) if back else g

            q3, k3, b3, v3 = (x.reshape(SB // SUB, SUB, w) for x in (q, key, b, v))
            acc = _dot((q * key).astype(BF16), head_ones) * v
            for dist in range(1, SUB):
                sft = (SUB - dist) if back else dist
                valid = (sub3 < SUB - dist) if back else (sub3 >= dist)
                e = jnp.where(valid, q3 * pltpu.roll(k3, sft, axis=1) * jnp.exp2(b3 - pltpu.roll(b3, sft, axis=1)), 0.0)
                red = _dot(e.reshape(SB, w).astype(BF16), head_ones)
                acc = acc + red * pltpu.roll(v3, sft, axis=1).reshape(SB, w)
            oacc[direction, rows, :] = acc

            for li, m in enumerate(LEVELS):
                lower, upper = _level_spans(m, False), _level_spans(m, True)
                qs, ks = (lower, upper) if back else (upper, lower)
                edge = [(a + m) if back else (a + m - 1) for a, _ in lower]
                bm = jnp.concatenate([jnp.broadcast_to(b[r:r + 1], (m, w)) for r in edge], axis=0)
                qg = _take(q, qs) * jnp.exp2(_take(b, qs) - bm)
                kg = (_take(key, ks) * jnp.exp2(bm - _take(b, ks))).astype(BF16)
                stacked = jnp.concatenate([jnp.where(in_head[h], qg, 0.0) for h in range(H_B)], axis=0)
                p = lax.dot_general(stacked.astype(BF16), kg, nt, preferred_element_type=F32)
                if m < NQ:
                    p = p * lm_ref[li]
                r = _dot(p.astype(BF16), _take(vb, ks))
                res = jnp.where(in_head[0], r[0:NQ], 0.0)
                for h in range(1, H_B):
                    res = res + jnp.where(in_head[h], r[h * NQ:(h + 1) * NQ], 0.0)
                for bi, (a0, _) in enumerate(qs):
                    dst = pl.ds(base + a0, m)
                    oacc[direction, dst, :] = oacc[direction, dst, :] + res[bi * m:(bi + 1) * m]

            cq, ck = (tot, 0.0) if back else (0.0, tot)
            kp = (key * jnp.exp2(ck - b)).astype(BF16)
            if with_inter:
                s_old = st[direction]
                qp = (q * jnp.exp2(b + cq)).astype(BF16)
                oacc[direction, rows, :] = oacc[direction, rows, :] + lax.dot_general(
                    qp, s_old.astype(BF16), nt, preferred_element_type=F32)
                kv_t = lax.dot_general(vb, kp, tn, preferred_element_type=F32)
                st[direction] = s_old * jnp.exp2(tot) + kv_t * hm_ref[...]
            else:
                kv = lax.dot_general(kp, vb, tn, preferred_element_type=F32)
                for h, blk in enumerate(head_blocks):
                    sfin_ref[(n_sb - 1 - si) if back else si, direction, h] = kv[blk, blk]
        return carry

    lax.fori_loop(0, n_sb, sb_body, 0, unroll=2 if n_sb % 2 == 0 else 1)

    def out_body(i, carry):
        rows = pl.ds(pl.multiple_of(i * SB, SB), SB)
        o = oacc[0, rows, :] + oacc[1, rows, :]
        o2 = o * o
        hi = o2.astype(BF16)
        lo = (o2 - hi.astype(F32)).astype(BF16)
        ms = (_dot(hi, head_ones) + _dot(lo, head_ones)) * (1.0 / HGRN_HD)
        o_ref[rows, :] = (o * lax.rsqrt(ms + EPS) * ng_ref[0] * sg_ref[rows, :]).astype(BF16)
        return carry

    lax.fori_loop(0, n_sb, out_body, 0)


def _hgrn(z, norm_g, head_mask, level_masks, s0, layer):
    t = z.shape[0]
    if s0 is None:
        assert CTX_SEQ == SB
        n_seq, seq = t // CTX_SEQ, CTX_SEQ * CTX_PER_STEP
        state_block = (CTX_PER_STEP, 2, H_B, HGRN_HD, HGRN_HD)
    else:
        n_seq, seq = t // LAT_SEQ, LAT_SEQ
    zspec = lambda col: pl.BlockSpec((seq, W_BR), lambda i, col=col: (i, col))
    wb = H_B * HGRN_HD
    in_specs = [zspec(1), zspec(2), zspec(3), zspec(4), zspec(5),
                _layer_spec(norm_g, layer),
                pl.BlockSpec((wb, wb), lambda i: (0, 0)),
                pl.BlockSpec(level_masks.shape, lambda i: (0, 0, 0))]
    args = [z, z, z, z, z, norm_g, head_mask, level_masks]
    out_specs = [pl.BlockSpec((seq, wb), lambda i: (i, 0))]
    out_shape = [jax.ShapeDtypeStruct((t, wb), BF16)]
    if s0 is not None:
        in_specs.append(pl.BlockSpec((1, 1, 2, H_B, HGRN_HD, HGRN_HD), lambda i: (i, layer, 0, 0, 0, 0)))
        args.append(s0)
    else:
        out_specs.append(pl.BlockSpec(state_block, lambda i: (i, 0, 0, 0, 0)))
        out_shape.append(jax.ShapeDtypeStruct((n_seq, 2, H_B, HGRN_HD, HGRN_HD), F32))
    return pl.pallas_call(
        functools.partial(_hgrn_kernel, has_s0=s0 is not None),
        grid=(t // seq,),
        in_specs=in_specs,
        out_specs=out_specs,
        out_shape=out_shape,
        scratch_shapes=[pltpu.VMEM((2, seq, wb), F32), pltpu.VMEM((2, wb, wb), F32)],
        compiler_params=_cparams(),
        name="hgrn_lat" if s0 is not None else "hgrn_ctx",
    )(*args)


TILE_ROWS = MERGE_TM // GRID_W
SUB_ROWS = MERGE_SUB // GRID_W


def _mix_ctx_segment(ua_ref, cb_ref, cc_ref, d_ref, rows, prm, pos, posl, tiles):
    wa_ref, ba_ref, lg_ref, lbeta_ref, wc_ref = prm
    seg = CTX_SEQ
    u_a = _conv_a_ln(ua_ref[rows, :], wa_ref, ba_ref, lg_ref, lbeta_ref).astype(BF16)
    cc = cc_ref[rows, :]
    u = (_shift_rows(cc, -1, pos, seg) * wc_ref[0:1, :] + cc * wc_ref[1:2, :]
         + _shift_rows(cc, 1, pos, seg) * wc_ref[2:3, :])
    u_c = (cb_ref[rows, :] * u).astype(BF16)
    parts = []
    for lanes, half, reach in tiles:
        dd = d_ref[rows, lanes]
        acc = jnp.zeros_like(dd)
        cnt = jnp.zeros_like(dd)
        for j in range(-reach, reach):
            lane_on = (j >= -half) & (j <= half - 1)
            valid = (posl + j >= 0) & (posl + j < seg) & lane_on
            acc = acc + jnp.where(valid, _shift_rows(dd, j, posl, seg), 0.0)
            cnt = cnt + jnp.where(valid, 1.0, 0.0)
        parts.append((acc / cnt - dd).astype(BF16))
    return u_a, u_c, jnp.concatenate(parts, axis=1)


def _mix_lat_rows(ua_ref, cb_ref, cc3, d3, first, seq_row0, prm, tiles):
    wa_ref, ba_ref, lg_ref, lbeta_ref, wc_ref = prm
    n_rows = LAT_SEQ // GRID_W

    def grid_row(refs, rr, lanes=slice(None)):
        ref, idx = (refs[0], rr + TILE_ROWS) if rr < 0 else (refs[2], rr - TILE_ROWS) if rr >= TILE_ROWS \
            else (refs[1], rr)
        ok = jnp.where(jnp.logical_and(seq_row0 + rr >= 0, seq_row0 + rr < n_rows), 1.0, 0.0)
        return ref[idx * GRID_W:(idx + 1) * GRID_W, lanes], ok

    out_a, out_c, out_d = [], [], []
    for r in range(first, first + SUB_ROWS):
        rows = slice(r * GRID_W, (r + 1) * GRID_W)
        out_a.append(_conv_a_ln(ua_ref[rows, :], wa_ref, ba_ref, lg_ref, lbeta_ref).astype(BF16))
        up, ok_up = grid_row(cc3, r - 1)
        dn, ok_dn = grid_row(cc3, r + 1)
        u = up * (wc_ref[0:1, :] * ok_up) + cc3[1][rows, :] * wc_ref[1:2, :] + dn * (wc_ref[2:3, :] * ok_dn)
        out_c.append((cb_ref[rows, :] * u).astype(BF16))
        parts = []
        for lanes, half, reach in tiles:
            acc = jnp.zeros((GRID_W, LANE_TILE), F32)
            cnt = jnp.zeros((1, LANE_TILE), F32)
            for j in range(-reach, reach):
                lane_on = jnp.where((j >= -half) & (j <= half - 1), 1.0, 0.0)
                x, ok = grid_row(d3, r + j, lanes)
                wgt = lane_on * ok
                acc = acc + x * wgt
                cnt = cnt + wgt
            parts.append((acc / cnt - d3[1][rows, lanes]).astype(BF16))
        out_d.append(jnp.concatenate(parts, axis=1))
    cat = lambda xs: jnp.concatenate(xs, axis=0)
    return cat(out_a), cat(out_c), cat(out_d)


def _mix_merge_kernel(*refs, latent):
    if latent:
        (h_ref, ua_ref, cb_ref, ccp_ref, cc_ref, ccn_ref, dp_ref, d_ref, dn_ref, ub_ref, x_ref, mod_ref, wg_ref,
         *prm, wa_ref, wb_ref, wc_ref, wd_ref, ps_ref, wo_ref, g2_ref, x1_ref, h2_ref) = refs
    else:
        (h_ref, ua_ref, cb_ref, cc_ref, d_ref, ub_ref, x_ref, mod_ref, wg_ref,
         *prm, wa_ref, wb_ref, wc_ref, wd_ref, ps_ref, wo_ref, g2_ref, x1_ref, h2_ref) = refs
    prm = [r.at[0] for r in prm]
    d = D_MODEL
    mod = mod_ref[0]
    g_m, sh_f, sc_f = mod[:, 2 * d:3 * d], mod[:, 3 * d:4 * d], mod[:, 4 * d:5 * d]
    tiles = _pool_lane_tiles()
    if latent:
        seq_row0 = (pl.program_id(0) % (LAT_SEQ // MERGE_TM)) * TILE_ROWS
    else:
        pos = _row_index(CTX_SEQ)
        posl = lax.broadcasted_iota(jnp.int32, (CTX_SEQ, LANE_TILE), 0)
    for s, r0 in enumerate(range(0, x_ref.shape[0], MERGE_SUB)):
        rows = slice(r0, r0 + MERGE_SUB)
        if latent:
            u_a, u_c, u_d = _mix_lat_rows(ua_ref, cb_ref, (ccp_ref, cc_ref, ccn_ref), (dp_ref, d_ref, dn_ref),
                                          s * SUB_ROWS, seq_row0, prm, tiles)
        else:
            u_a, u_c, u_d = _mix_ctx_segment(ua_ref, cb_ref, cc_ref, d_ref, rows, prm, pos, posl, tiles)
        h = h_ref[rows, :]
        merged = None
        for i, (u, w_ref) in enumerate(((u_a, wa_ref), (ub_ref[rows, :], wb_ref), (u_c, wc_ref), (u_d, wd_ref))):
            y = _dot(u, w_ref[0])
            if i == 3:
                y = y * ps_ref[0]
            term = _sigmoid(_dot(h, wg_ref[0, :, i * d:(i + 1) * d])) * y
            merged = term if merged is None else merged + term
        mix = _dot(merged.astype(BF16), wo_ref[0])
        x1 = x_ref[rows, :] + g_m * mix
        x1_ref[rows, :] = x1
        h2_ref[rows, :] = (_rmsnorm(x1, g2_ref[0]) * (1.0 + sc_f) + sh_f).astype(BF16)


def _mix_merge(h, z, u_b, x, mod, mod_row, w_gate, w, layer, tokens_per_mod, latent):
    t, d = x.shape
    tm = MERGE_TM
    n_tiles = t // tm
    assert MERGE_SUB == CTX_SEQ and TILE_ROWS >= max(POOL_WINDOWS) // 2 and LAT_SEQ % tm == 0
    tok = lambda width: pl.BlockSpec((tm, width), lambda i: (i, 0))
    zcol = lambda col: pl.BlockSpec((tm, W_BR), lambda i, col=col: (i, col))
    above = lambda col: pl.BlockSpec((tm, W_BR), lambda i, col=col: (jnp.maximum(i - 1, 0), col))
    below = lambda col: pl.BlockSpec((tm, W_BR), lambda i, col=col: (jnp.minimum(i + 1, n_tiles - 1), col))
    if latent:
        z_specs = [zcol(0), zcol(6), above(7), zcol(7), below(7), above(8), zcol(8), below(8)]
    else:
        z_specs = [zcol(0), zcol(6), zcol(7), zcol(8)]
    conv = [w["conv_a_w"], w["conv_a_b"], w["ln_a_g"], w["ln_a_b"], w["conv_c_w"]]
    weights = [w["w_out_a"], w["w_out_b"], w["w_out_c"], w["w_pool"], w["pool_scale"], w["w_o"], w["norm_ffn_g"]]
    return pl.pallas_call(
        functools.partial(_mix_merge_kernel, latent=latent),
        grid=(n_tiles,),
        in_specs=[tok(d)] + z_specs + [tok(W_BR), tok(d), _mod_spec(tm, tokens_per_mod, mod_row),
                                       pl.BlockSpec((pl.Element(1), pl.Element(d), pl.Element(N_BRANCH * d)),
                                                    lambda i: (layer, 0, N_MIX))]
                 + [_layer_spec(a, layer) for a in conv + weights],
        out_specs=[tok(d), tok(d)],
        out_shape=[jax.ShapeDtypeStruct((t, d), F32), jax.ShapeDtypeStruct((t, d), BF16)],
        compiler_params=_cparams(),
        name="merge_lat" if latent else "merge_ctx",
    )(h, *([z] * len(z_specs)), u_b, x, mod, w_gate, *conv, *weights)


def _ffn_kernel(h2_ref, x1_ref, mod_ref, w13_ref, w2_ref, gf_ref, o_ref, *, final):
    d = D_MODEL
    w13_ref, w2_ref = w13_ref.at[0], w2_ref.at[0]
    g_f = mod_ref[0][:, 5 * d:6 * d]
    for r0 in range(0, x1_ref.shape[0], FFN_SUB):
        rows = slice(r0, r0 + FFN_SUB)
        h2 = h2_ref[rows, :]
        acc = None
        for lo in range(0, D_FF, FFN_CHUNK):
            hi = min(lo + FFN_CHUNK, D_FF)
            gate = _dot(h2, w13_ref[:, lo:hi])
            up = _dot(h2, w13_ref[:, D_FF + lo:D_FF + hi])
            part = _dot((_silu(gate) * up).astype(BF16), w2_ref[lo:hi, :])
            acc = part if acc is None else acc + part
        x2 = x1_ref[rows, :] + g_f * acc
        if final:
            x2 = _rmsnorm(x2, gf_ref[...])
        o_ref[rows, :] = x2


def _ffn(h2, x1, mod, mod_row, w13, w2, final_g, layer, tokens_per_mod, final):
    t, d = x1.shape
    tm = FFN_TM
    return pl.pallas_call(
        functools.partial(_ffn_kernel, final=final),
        grid=(t // tm,),
        in_specs=[pl.BlockSpec((tm, d), lambda i: (i, 0)), pl.BlockSpec((tm, d), lambda i: (i, 0)),
                  _mod_spec(tm, tokens_per_mod, mod_row), _layer_spec(w13, layer), _layer_spec(w2, layer),
                  pl.BlockSpec(final_g.shape, lambda i: (0, 0))],
        out_specs=pl.BlockSpec((tm, d), lambda i: (i, 0)),
        out_shape=jax.ShapeDtypeStruct((t, d), F32),
        compiler_params=_cparams(),
        name="ffn",
    )(h2, x1, mod, w13, w2, final_g)


def _block_diag_heads(s):
    eye = jnp.eye(H_B, dtype=s.dtype)
    bd = jnp.einsum("...hab,hg->...hagb", s, eye)
    return bd.reshape(s.shape[:-3] + (H_B * s.shape[-2], H_B * s.shape[-1]))


def kernel(x_prompt, x_sample, state_hgrn, c, c_ctx, ada_w, ada_b, norm_mix_g, w_in, conv_a_w, conv_a_b, ln_a_g,
           ln_a_b, w_out_a, hgrn_lb_logits, hgrn_norm_g, w_out_b, conv_c_w, w_out_c, pool_w, pool_scale, w_o,
           norm_ffn_g, ffn_w13, ffn_w2, final_norm_g):
    depth = w_in.shape[0]
    d = D_MODEL
    n_ctx, n_lat = x_prompt.shape[0], x_sample.shape[0]
    t_ctx, t_lat = n_ctx * CTX_SEQ, n_lat * LAT_SEQ

    n_rows = -(-(1 + n_lat) // 8) * 8
    c_rows = jnp.zeros((n_rows, d), F32).at[0].set(c_ctx).at[1:1 + n_lat].set(c)
    mod = _modulation(c_rows, ada_w, ada_b).reshape(depth * n_rows, 1, 6 * d)

    lb_all = jnp.cumsum(jax.nn.softmax(hgrn_lb_logits.astype(F32), axis=0), axis=0)
    lb_all = (lb_all - lb_all[:1]).reshape(depth, 1, 2 * W_BR)

    head_mask = _block_diag_heads(jnp.ones((H_B, HGRN_HD, HGRN_HD), F32))
    level_masks = jnp.asarray(_level_masks())

    rows = lambda a: a.reshape(depth, 1, -1)
    w = {
        "w_out_a": w_out_a.astype(BF16), "w_out_b": w_out_b.astype(BF16), "w_out_c": w_out_c.astype(BF16),
        "w_pool": _block_diag_heads(pool_w).astype(BF16), "pool_scale": rows(pool_scale),
        "w_o": w_o.astype(BF16), "norm_ffn_g": rows(norm_ffn_g),
        "conv_a_w": conv_a_w, "conv_a_b": rows(conv_a_b), "ln_a_g": rows(ln_a_g), "ln_a_b": rows(ln_a_b),
        "conv_c_w": conv_c_w,
    }
    norm_mix, hgrn_g = rows(norm_mix_g), rows(hgrn_norm_g)
    w13 = ffn_w13.astype(BF16)
    w2 = ffn_w2.astype(BF16)
    final_g = final_norm_g.reshape(1, d)

    y_p = x_prompt.reshape(t_ctx, d)
    y_s = x_sample.reshape(t_lat, d)
    ctx_states = []
    w_mix = w_gate = w_in.astype(BF16)
    for l in range(depth):
        final = l == depth - 1
        outs = []
        for latent, x, tokens_per_mod in ((False, y_p, t_ctx), (True, y_s, LAT_SEQ)):
            mod_row = l * n_rows + (1 if latent else 0)
            h, z = _inproj(x, mod, mod_row, norm_mix, w_mix, lb_all, l, tokens_per_mod)
            hg = _hgrn(z, hgrn_g, head_mask, level_masks, state_hgrn if latent else None, l)
            if not latent:
                ctx_states.append(hg[1])
            x1, h2 = _mix_merge(h, z, hg[0], x, mod, mod_row, w_gate, w, l, tokens_per_mod, latent)
            outs.append(_ffn(h2, x1, mod, mod_row, w13, w2, final_g, l, tokens_per_mod, final))
        y_p, y_s = outs

    new_state = jnp.stack(ctx_states, axis=1).astype(x_prompt.dtype)
    return (y_p.reshape(x_prompt.shape), y_s.reshape(x_sample.shape), new_state)
```

```python
import functools

import numpy as np
import jax
import jax.numpy as jnp
from jax import lax
from jax.experimental import pallas as pl
from jax.experimental.pallas import tpu as pltpu

F32 = jnp.float32
BF16 = jnp.bfloat16

D_MODEL = 1024
GRID_W = 64
EPS = 1e-6
W_BR = 256
CONV_A_K = 31
H_B = 4
HGRN_HD = 64
POOL_WINDOWS = (2, 4, 8, 16)
POOL_GC = 64
D_FF = 2816
N_MIX = 11 * W_BR
N_BRANCH = 4
N_Z = 9 * W_BR
CTX_SEQ = 256
LAT_SEQ = 2048

INPROJ_TM = 1024
MERGE_TM = 512
FFN_TM = 1024
FFN_CHUNK = 256
INPROJ_SUB = 128
MERGE_SUB = 256
FFN_SUB = 512

VMEM_LIMIT = 56 * 1024 * 1024
LOG2E = 1.4426950408889634


def _cparams(n_axes=1):
    return pltpu.CompilerParams(dimension_semantics=("arbitrary",) * n_axes, vmem_limit_bytes=VMEM_LIMIT)


def _sigmoid(x):
    return 1.0 / (1.0 + jnp.exp(-x))


def _silu(x):
    return x * _sigmoid(x)


def _rmsnorm(x, g):
    return x * lax.rsqrt(jnp.mean(x * x, axis=-1, keepdims=True) + EPS) * g


def _dot(a, b):
    return jnp.dot(a, b, preferred_element_type=F32)


def _layer_spec(a, layer):
    return pl.BlockSpec((1,) + a.shape[1:], lambda i: (layer,) + (0,) * (a.ndim - 1))


def _mod_spec(tm, tokens_per_mod, first_row):
    return pl.BlockSpec((1, 1, 6 * D_MODEL), lambda i: (first_row + (i * tm) // tokens_per_mod, 0, 0))


def _mod_kernel(c_ref, w_ref, b_ref, o_ref):
    s = _silu(c_ref[...]).astype(BF16)
    o_ref[0] = _dot(s, w_ref[0].astype(BF16)) + b_ref[0]


def _modulation(c_rows, ada_w, ada_b):
    depth, d, n = ada_w.shape
    nr = c_rows.shape[0]
    tn = 1536
    return pl.pallas_call(
        _mod_kernel,
        grid=(depth, n // tn),
        in_specs=[
            pl.BlockSpec((nr, d), lambda l, j: (0, 0)),
            pl.BlockSpec((1, d, tn), lambda l, j: (l, 0, j)),
            pl.BlockSpec((1, 1, tn), lambda l, j: (l, 0, j)),
        ],
        out_specs=pl.BlockSpec((1, nr, tn), lambda l, j: (l, 0, j)),
        out_shape=jax.ShapeDtypeStruct((depth, nr, n), F32),
        compiler_params=_cparams(2),
        name="modulation",
    )(c_rows, ada_w, ada_b.reshape(depth, 1, n))


def _log2_forget(z, lb):
    zs = z * LOG2E
    ls = jnp.minimum(zs, 0.0) - jnp.log2(1.0 + jnp.exp2(-jnp.abs(zs)))
    a = jnp.log2(lb)
    b = jnp.log1p(-lb) * LOG2E + ls
    return jnp.maximum(a, b) + jnp.log2(1.0 + jnp.exp2(-jnp.abs(a - b)))


def _inproj_kernel(x_ref, mod_ref, g_ref, w_ref, lb_ref, h_ref, z_ref):
    d, w = D_MODEL, W_BR
    mod = mod_ref[0]
    sh, sc = mod[:, 0:d], mod[:, d:2 * d]
    lb = lb_ref[0]
    for r0 in range(0, x_ref.shape[0], INPROJ_SUB):
        rows = slice(r0, r0 + INPROJ_SUB)
        h = (_rmsnorm(x_ref[rows, :], g_ref[0]) * (1.0 + sc) + sh).astype(BF16)
        h_ref[rows, :] = h

        def proj(j, n, h=h):
            return _dot(h, w_ref[:, j * w:(j + n) * w])

        za = proj(0, 2)
        z_ref[rows, 0 * w:1 * w] = za[:, 0:w] * _sigmoid(za[:, w:2 * w])
        zb = proj(2, 3)
        z_ref[rows, 1 * w:2 * w] = zb[:, 0:w]
        z_ref[rows, 2 * w:4 * w] = _log2_forget(zb[:, w:3 * w], lb)
        zv = proj(5, 2)
        z_ref[rows, 4 * w:5 * w] = zv[:, 0:w]
        z_ref[rows, 5 * w:6 * w] = _silu(zv[:, w:2 * w])
        zc = proj(7, 3)
        z_ref[rows, 6 * w:7 * w] = zc[:, 0:w]
        z_ref[rows, 7 * w:8 * w] = zc[:, w:2 * w] * zc[:, 2 * w:3 * w]
        z_ref[rows, 8 * w:9 * w] = proj(10, 1)


def _inproj(x, mod, mod_row, g, w_mix, lb, layer, tokens_per_mod):
    t, d = x.shape
    tm = INPROJ_TM
    return pl.pallas_call(
        _inproj_kernel,
        grid=(t // tm,),
        in_specs=[
            pl.BlockSpec((tm, d), lambda i: (i, 0)),
            _mod_spec(tm, tokens_per_mod, mod_row),
            _layer_spec(g, layer),
            pl.BlockSpec((pl.Squeezed(), d, N_MIX), lambda i: (layer, 0, 0)),
            _layer_spec(lb, layer),
        ],
        out_specs=[
            pl.BlockSpec((tm, d), lambda i: (i, 0)),
            pl.BlockSpec((tm, N_Z), lambda i: (i, 0)),
        ],
        out_shape=[jax.ShapeDtypeStruct((t, d), BF16), jax.ShapeDtypeStruct((t, N_Z), F32)],
        compiler_params=_cparams(),
        name="inproj",
    )(x, mod, g, w_mix, lb)


def _row_index(rows):
    return lax.broadcasted_iota(jnp.int32, (rows, W_BR), 0)


def _shift_rows(x, j, pos, seg):
    if j == 0:
        return x
    y = pltpu.roll(x, (-j) % x.shape[0], axis=0)
    valid = (pos < seg - j) if j > 0 else (pos >= -j)
    return jnp.where(valid, y, 0.0)


def _conv_a_ln(x, wa_ref, ba_ref, lg_ref, lbeta_ref):
    half = CONV_A_K // 2
    seg = x.shape[0]
    pad = 2 * SUBLANES
    zeros = jnp.zeros((pad, x.shape[1]), F32)
    xpad = jnp.concatenate([zeros, x, zeros], axis=0)
    n = seg + 2 * pad
    rotated = [xpad] + [pltpu.roll(xpad, n - r, axis=0) for r in range(1, SUBLANES)]
    acc = None
    for j in range(-half, half + 1):
        whole, r = divmod(pad + j, SUBLANES)
        term = rotated[r][whole * SUBLANES:whole * SUBLANES + seg] * wa_ref[half + j:half + j + 1, :]
        acc = term if acc is None else acc + term
    u = acc + ba_ref[...]
    mu = jnp.mean(u, axis=-1, keepdims=True)
    var = jnp.mean(jnp.square(u - mu), axis=-1, keepdims=True)
    return _silu((u - mu) * lax.rsqrt(var + EPS) * lg_ref[...] + lbeta_ref[...])


LANE_TILE = 128
SUBLANES = 8


def _pool_lane_tiles():
    per_tile = LANE_TILE // POOL_GC
    tiles = []
    for i in range(W_BR // LANE_TILE):
        lane = lax.broadcasted_iota(jnp.int32, (1, LANE_TILE), 1) + i * LANE_TILE
        half = jnp.left_shift(1, lane // POOL_GC)
        reach = max(POOL_WINDOWS[i * per_tile:(i + 1) * per_tile]) // 2
        tiles.append((slice(i * LANE_TILE, (i + 1) * LANE_TILE), half, reach))
    return tiles


SB = 256
SUB = 8
LEVELS = (8, 16, 32, 64, 128)
NQ = SB // 2
CTX_PER_STEP = 2


def _level_spans(m, upper):
    off = m if upper else 0
    return [(i * 2 * m + off, i * 2 * m + off + m) for i in range(SB // (2 * m))]


def _take(x, spans):
    parts = [x[a:b] for a, b in spans]
    return parts[0] if len(parts) == 1 else jnp.concatenate(parts, axis=0)


def _level_masks():
    tq = np.arange(H_B * NQ)[:, None] % NQ
    ts = np.arange(NQ)[None, :]
    return np.stack([((tq // m) == (ts // m)) for m in LEVELS[:-1]]).astype(np.float32)


def _cumsum_rows(x):
    g = SB // SUB
    x3 = x.reshape(g, SUB, x.shape[-1])
    sub = lax.broadcasted_iota(jnp.int32, x3.shape, 1)
    step = 1
    while step < SUB:
        x3 = x3 + jnp.where(sub >= step, pltpu.roll(x3, step, axis=1), 0.0)
        step *= 2
    tot = x3[:, SUB - 1:SUB, :]
    offs = [jnp.zeros_like(tot[0:1])]
    for i in range(1, g):
        offs.append(offs[-1] + tot[i - 1:i])
    return (x3 + jnp.concatenate(offs, axis=0)).reshape(x.shape)


def _hgrn_kernel(*refs, has_s0):
    if has_s0:
        q_ref, lff_ref, lfb_ref, v_ref, sg_ref, ng_ref, hm_ref, lm_ref, s0_ref, o_ref, oacc, st = refs
        sfin_ref = None
    else:
        q_ref, lff_ref, lfb_ref, v_ref, sg_ref, ng_ref, hm_ref, lm_ref, o_ref, sfin_ref, oacc, st = refs
        s0_ref = None
    seq, w = q_ref.shape
    n_sb = seq // SB
    with_inter = has_s0
    head_ones = hm_ref[...].astype(BF16)
    lane_head = lax.broadcasted_iota(jnp.int32, (1, w), 1) // HGRN_HD
    in_head = [lane_head == h for h in range(H_B)]
    sub3 = lax.broadcasted_iota(jnp.int32, (SB // SUB, SUB, w), 1)
    nt = (((1,), (1,)), ((), ()))
    tn = (((0,), (0,)), ((), ()))
    head_blocks = [slice(h * HGRN_HD, (h + 1) * HGRN_HD) for h in range(H_B)]

    for direction in range(2 if with_inter else 0):
        st[direction] = jnp.zeros((w, w), F32)
        if s0_ref is not None:
            for h, blk in enumerate(head_blocks):
                st[direction, blk, blk] = s0_ref[0, 0, direction, h]
            st[direction] = st[direction].T

    def sb_body(si, carry):
        for direction, lf_ref in ((0, lff_ref), (1, lfb_ref)):
            back = direction == 1
            base = pl.multiple_of(((n_sb - 1 - si) if back else si) * SB, SB)
            rows = pl.ds(base, SB)
            lf = lf_ref[rows, :]
            q = q_ref[rows, :]
            v = v_ref[rows, :]
            vb = v.astype(BF16)
            key = 1.0 - jnp.exp2(lf)
            g = _cumsum_rows(lf)
            tot = g[SB - 1:SB, :]
            b = (lf - g) if back else g

            q3, k3, b3, v3 = (x.reshape(SB // SUB, SUB, w) for x in (q, key, b, v))
            acc = _dot((q * key).astype(BF16), head_ones) * v
            for dist in range(1, SUB):
                sft = (SUB - dist) if back else dist
                valid = (sub3 < SUB - dist) if back else (sub3 >= dist)
                e = jnp.where(valid, q3 * pltpu.roll(k3, sft, axis=1) * jnp.exp2(b3 - pltpu.roll(b3, sft, axis=1)), 0.0)
                red = _dot(e.reshape(SB, w).astype(BF16), head_ones)
                acc = acc + red * pltpu.roll(v3, sft, axis=1).reshape(SB, w)
            oacc[direction, rows, :] = acc

            for li, m in enumerate(LEVELS):
                lower, upper = _level_spans(m, False), _level_spans(m, True)
                qs, ks = (lower, upper) if back else (upper, lower)
                edge = [(a + m) if back else (a + m - 1) for a, _ in lower]
                bm = jnp.concatenate([jnp.broadcast_to(b[r:r + 1], (m, w)) for r in edge], axis=0)
                qg = _take(q, qs) * jnp.exp2(_take(b, qs) - bm)
                kg = (_take(key, ks) * jnp.exp2(bm - _take(b, ks))).astype(BF16)
                stacked = jnp.concatenate([jnp.where(in_head[h], qg, 0.0) for h in range(H_B)], axis=0)
                p = lax.dot_general(stacked.astype(BF16), kg, nt, preferred_element_type=F32)
                if m < NQ:
                    p = p * lm_ref[li]
                r = _dot(p.astype(BF16), _take(vb, ks))
                res = jnp.where(in_head[0], r[0:NQ], 0.0)
                for h in range(1, H_B):
                    res = res + jnp.where(in_head[h], r[h * NQ:(h + 1) * NQ], 0.0)
                for bi, (a0, _) in enumerate(qs):
                    dst = pl.ds(base + a0, m)
                    oacc[direction, dst, :] = oacc[direction, dst, :] + res[bi * m:(bi + 1) * m]

            cq, ck = (tot, 0.0) if back else (0.0, tot)
            kp = (key * jnp.exp2(ck - b)).astype(BF16)
            if with_inter:
                s_old = st[direction]
                qp = (q * jnp.exp2(b + cq)).astype(BF16)
                oacc[direction, rows, :] = oacc[direction, rows, :] + lax.dot_general(
                    qp, s_old.astype(BF16), nt, preferred_element_type=F32)
                kv_t = lax.dot_general(vb, kp, tn, preferred_element_type=F32)
                st[direction] = s_old * jnp.exp2(tot) + kv_t * hm_ref[...]
            else:
                kv = lax.dot_general(kp, vb, tn, preferred_element_type=F32)
                for h, blk in enumerate(head_blocks):
                    sfin_ref[(n_sb - 1 - si) if back else si, direction, h] = kv[blk, blk]
        return carry

    lax.fori_loop(0, n_sb, sb_body, 0, unroll=2 if n_sb % 2 == 0 else 1)

    def out_body(i, carry):
        rows = pl.ds(pl.multiple_of(i * SB, SB), SB)
        o = oacc[0, rows, :] + oacc[1, rows, :]
        o2 = o * o
        hi = o2.astype(BF16)
        lo = (o2 - hi.astype(F32)).astype(BF16)
        ms = (_dot(hi, head_ones) + _dot(lo, head_ones)) * (1.0 / HGRN_HD)
        o_ref[rows, :] = (o * lax.rsqrt(ms + EPS) * ng_ref[0] * sg_ref[rows, :]).astype(BF16)
        return carry

    lax.fori_loop(0, n_sb, out_body, 0)


def _hgrn(z, norm_g, head_mask, level_masks, s0, layer):
    t = z.shape[0]
    if s0 is None:
        assert CTX_SEQ == SB
        n_seq, seq = t // CTX_SEQ, CTX_SEQ * CTX_PER_STEP
        state_block = (CTX_PER_STEP, 2, H_B, HGRN_HD, HGRN_HD)
    else:
        n_seq, seq = t // LAT_SEQ, LAT_SEQ
    zspec = lambda col: pl.BlockSpec((seq, W_BR), lambda i, col=col: (i, col))
    wb = H_B * HGRN_HD
    in_specs = [zspec(1), zspec(2), zspec(3), zspec(4), zspec(5),
                _layer_spec(norm_g, layer),
                pl.BlockSpec((wb, wb), lambda i: (0, 0)),
                pl.BlockSpec(level_masks.shape, lambda i: (0, 0, 0))]
    args = [z, z, z, z, z, norm_g, head_mask, level_masks]
    out_specs = [pl.BlockSpec((seq, wb), lambda i: (i, 0))]
    out_shape = [jax.ShapeDtypeStruct((t, wb), BF16)]
    if s0 is not None:
        in_specs.append(pl.BlockSpec((1, 1, 2, H_B, HGRN_HD, HGRN_HD), lambda i: (i, layer, 0, 0, 0, 0)))
        args.append(s0)
    else:
        out_specs.append(pl.BlockSpec(state_block, lambda i: (i, 0, 0, 0, 0)))
        out_shape.append(jax.ShapeDtypeStruct((n_seq, 2, H_B, HGRN_HD, HGRN_HD), F32))
    return pl.pallas_call(
        functools.partial(_hgrn_kernel, has_s0=s0 is not None),
        grid=(t // seq,),
        in_specs=in_specs,
        out_specs=out_specs,
        out_shape=out_shape,
        scratch_shapes=[pltpu.VMEM((2, seq, wb), F32), pltpu.VMEM((2, wb, wb), F32)],
        compiler_params=_cparams(),
        name="hgrn_lat" if s0 is not None else "hgrn_ctx",
    )(*args)


TILE_ROWS = MERGE_TM // GRID_W
SUB_ROWS = MERGE_SUB // GRID_W


def _mix_ctx_segment(ua_ref, cb_ref, cc_ref, d_ref, rows, prm, pos, posl, tiles):
    wa_ref, ba_ref, lg_ref, lbeta_ref, wc_ref = prm
    seg = CTX_SEQ
    u_a = _conv_a_ln(ua_ref[rows, :], wa_ref, ba_ref, lg_ref, lbeta_ref).astype(BF16)
    cc = cc_ref[rows, :]
    u = (_shift_rows(cc, -1, pos, seg) * wc_ref[0:1, :] + cc * wc_ref[1:2, :]
         + _shift_rows(cc, 1, pos, seg) * wc_ref[2:3, :])
    u_c = (cb_ref[rows, :] * u).astype(BF16)
    parts = []
    for lanes, half, reach in tiles:
        dd = d_ref[rows, lanes]
        acc = jnp.zeros_like(dd)
        cnt = jnp.zeros_like(dd)
        for j in range(-reach, reach):
            lane_on = (j >= -half) & (j <= half - 1)
            valid = (posl + j >= 0) & (posl + j < seg) & lane_on
            acc = acc + jnp.where(valid, _shift_rows(dd, j, posl, seg), 0.0)
            cnt = cnt + jnp.where(valid, 1.0, 0.0)
        parts.append((acc / cnt - dd).astype(BF16))
    return u_a, u_c, jnp.concatenate(parts, axis=1)


def _mix_lat_rows(ua_ref, cb_ref, cc3, d3, first, seq_row0, prm, tiles):
    wa_ref, ba_ref, lg_ref, lbeta_ref, wc_ref = prm
    n_rows = LAT_SEQ // GRID_W

    def grid_row(refs, rr, lanes=slice(None)):
        ref, idx = (refs[0], rr + TILE_ROWS) if rr < 0 else (refs[2], rr - TILE_ROWS) if rr >= TILE_ROWS \
            else (refs[1], rr)
        ok = jnp.where(jnp.logical_and(seq_row0 + rr >= 0, seq_row0 + rr < n_rows), 1.0, 0.0)
        return ref[idx * GRID_W:(idx + 1) * GRID_W, lanes], ok

    out_a, out_c, out_d = [], [], []
    for r in range(first, first + SUB_ROWS):
        rows = slice(r * GRID_W, (r + 1) * GRID_W)
        out_a.append(_conv_a_ln(ua_ref[rows, :], wa_ref, ba_ref, lg_ref, lbeta_ref).astype(BF16))
        up, ok_up = grid_row(cc3, r - 1)
        dn, ok_dn = grid_row(cc3, r + 1)
        u = up * (wc_ref[0:1, :] * ok_up) + cc3[1][rows, :] * wc_ref[1:2, :] + dn * (wc_ref[2:3, :] * ok_dn)
        out_c.append((cb_ref[rows, :] * u).astype(BF16))
        parts = []
        for lanes, half, reach in tiles:
            acc = jnp.zeros((GRID_W, LANE_TILE), F32)
            cnt = jnp.zeros((1, LANE_TILE), F32)
            for j in range(-reach, reach):
                lane_on = jnp.where((j >= -half) & (j <= half - 1), 1.0, 0.0)
                x, ok = grid_row(d3, r + j, lanes)
                wgt = lane_on * ok
                acc = acc + x * wgt
                cnt = cnt + wgt
            parts.append((acc / cnt - d3[1][rows, lanes]).astype(BF16))
        out_d.append(jnp.concatenate(parts, axis=1))
    cat = lambda xs: jnp.concatenate(xs, axis=0)
    return cat(out_a), cat(out_c), cat(out_d)


def _mix_merge_kernel(*refs, latent):
    if latent:
        (h_ref, ua_ref, cb_ref, ccp_ref, cc_ref, ccn_ref, dp_ref, d_ref, dn_ref, ub_ref, x_ref, mod_ref, wg_ref,
         *prm, wa_ref, wb_ref, wc_ref, wd_ref, ps_ref, wo_ref, g2_ref, x1_ref, h2_ref) = refs
    else:
        (h_ref, ua_ref, cb_ref, cc_ref, d_ref, ub_ref, x_ref, mod_ref, wg_ref,
         *prm, wa_ref, wb_ref, wc_ref, wd_ref, ps_ref, wo_ref, g2_ref, x1_ref, h2_ref) = refs
    prm = [r.at[0] for r in prm]
    d = D_MODEL
    mod = mod_ref[0]
    g_m, sh_f, sc_f = mod[:, 2 * d:3 * d], mod[:, 3 * d:4 * d], mod[:, 4 * d:5 * d]
    tiles = _pool_lane_tiles()
    if latent:
        seq_row0 = (pl.program_id(0) % (LAT_SEQ // MERGE_TM)) * TILE_ROWS
    else:
        pos = _row_index(CTX_SEQ)
        posl = lax.broadcasted_iota(jnp.int32, (CTX_SEQ, LANE_TILE), 0)
    for s, r0 in enumerate(range(0, x_ref.shape[0], MERGE_SUB)):
        rows = slice(r0, r0 + MERGE_SUB)
        if latent:
            u_a, u_c, u_d = _mix_lat_rows(ua_ref, cb_ref, (ccp_ref, cc_ref, ccn_ref), (dp_ref, d_ref, dn_ref),
                                          s * SUB_ROWS, seq_row0, prm, tiles)
        else:
            u_a, u_c, u_d = _mix_ctx_segment(ua_ref, cb_ref, cc_ref, d_ref, rows, prm, pos, posl, tiles)
        h = h_ref[rows, :]
        merged = None
        for i, u, w_ref in ((1, ub_ref[rows, :], wb_ref), (0, u_a, wa_ref), (2, u_c, wc_ref), (3, u_d, wd_ref)):
            y = _dot(u, w_ref[0])
            if i == 3:
                y = y * ps_ref[0]
            term = _sigmoid(_dot(h, wg_ref[0, :, i * d:(i + 1) * d])) * y
            merged = term if merged is None else merged + term
        mix = _dot(merged.astype(BF16), wo_ref[0])
        x1 = x_ref[rows, :] + g_m * mix
        x1_ref[rows, :] = x1
        h2_ref[rows, :] = (_rmsnorm(x1, g2_ref[0]) * (1.0 + sc_f) + sh_f).astype(BF16)


def _mix_merge(h, z, u_b, x, mod, mod_row, w_gate, w, layer, tokens_per_mod, latent):
    t, d = x.shape
    tm = MERGE_TM
    n_tiles = t // tm
    assert MERGE_SUB == CTX_SEQ and TILE_ROWS >= max(POOL_WINDOWS) // 2 and LAT_SEQ % tm == 0
    tok = lambda width: pl.BlockSpec((tm, width), lambda i: (i, 0))
    zcol = lambda col: pl.BlockSpec((tm, W_BR), lambda i, col=col: (i, col))
    above = lambda col: pl.BlockSpec((tm, W_BR), lambda i, col=col: (jnp.maximum(i - 1, 0), col))
    below = lambda col: pl.BlockSpec((tm, W_BR), lambda i, col=col: (jnp.minimum(i + 1, n_tiles - 1), col))
    if latent:
        z_specs = [zcol(0), zcol(6), above(7), zcol(7), below(7), above(8), zcol(8), below(8)]
    else:
        z_specs = [zcol(0), zcol(6), zcol(7), zcol(8)]
    conv = [w["conv_a_w"], w["conv_a_b"], w["ln_a_g"], w["ln_a_b"], w["conv_c_w"]]
    weights = [w["w_out_a"], w["w_out_b"], w["w_out_c"], w["w_pool"], w["pool_scale"], w["w_o"], w["norm_ffn_g"]]
    return pl.pallas_call(
        functools.partial(_mix_merge_kernel, latent=latent),
        grid=(n_tiles,),
        in_specs=[tok(d)] + z_specs + [tok(W_BR), tok(d), _mod_spec(tm, tokens_per_mod, mod_row),
                                       pl.BlockSpec((pl.Element(1), pl.Element(d), pl.Element(N_BRANCH * d)),
                                                    lambda i: (layer, 0, N_MIX))]
                 + [_layer_spec(a, layer) for a in conv + weights],
        out_specs=[tok(d), tok(d)],
        out_shape=[jax.ShapeDtypeStruct((t, d), F32), jax.ShapeDtypeStruct((t, d), BF16)],
        compiler_params=_cparams(),
        name="merge_lat" if latent else "merge_ctx",
    )(h, *([z] * len(z_specs)), u_b, x, mod, w_gate, *conv, *weights)


def _ffn_kernel(h2_ref, x1_ref, mod_ref, w13_ref, w2_ref, gf_ref, o_ref, *, final):
    d = D_MODEL
    w13_ref, w2_ref = w13_ref.at[0], w2_ref.at[0]
    g_f = mod_ref[0][:, 5 * d:6 * d]
    for r0 in range(0, x1_ref.shape[0], FFN_SUB):
        rows = slice(r0, r0 + FFN_SUB)
        h2 = h2_ref[rows, :]
        acc = None
        for lo in range(0, D_FF, FFN_CHUNK):
            hi = min(lo + FFN_CHUNK, D_FF)
            gate = _dot(h2, w13_ref[:, lo:hi])
            up = _dot(h2, w13_ref[:, D_FF + lo:D_FF + hi])
            part = _dot((_silu(gate) * up).astype(BF16), w2_ref[lo:hi, :])
            acc = part if acc is None else acc + part
        x2 = x1_ref[rows, :] + g_f * acc
        if final:
            x2 = _rmsnorm(x2, gf_ref[...])
        o_ref[rows, :] = x2


def _ffn(h2, x1, mod, mod_row, w13, w2, final_g, layer, tokens_per_mod, final):
    t, d = x1.shape
    tm = FFN_TM
    return pl.pallas_call(
        functools.partial(_ffn_kernel, final=final),
        grid=(t // tm,),
        in_specs=[pl.BlockSpec((tm, d), lambda i: (i, 0)), pl.BlockSpec((tm, d), lambda i: (i, 0)),
                  _mod_spec(tm, tokens_per_mod, mod_row), _layer_spec(w13, layer), _layer_spec(w2, layer),
                  pl.BlockSpec(final_g.shape, lambda i: (0, 0))],
        out_specs=pl.BlockSpec((tm, d), lambda i: (i, 0)),
        out_shape=jax.ShapeDtypeStruct((t, d), F32),
        compiler_params=_cparams(),
        name="ffn",
    )(h2, x1, mod, w13, w2, final_g)


def _block_diag_heads(s):
    eye = jnp.eye(H_B, dtype=s.dtype)
    bd = jnp.einsum("...hab,hg->...hagb", s, eye)
    return bd.reshape(s.shape[:-3] + (H_B * s.shape[-2], H_B * s.shape[-1]))


def kernel(x_prompt, x_sample, state_hgrn, c, c_ctx, ada_w, ada_b, norm_mix_g, w_in, conv_a_w, conv_a_b, ln_a_g,
           ln_a_b, w_out_a, hgrn_lb_logits, hgrn_norm_g, w_out_b, conv_c_w, w_out_c, pool_w, pool_scale, w_o,
           norm_ffn_g, ffn_w13, ffn_w2, final_norm_g):
    depth = w_in.shape[0]
    d = D_MODEL
    n_ctx, n_lat = x_prompt.shape[0], x_sample.shape[0]
    t_ctx, t_lat = n_ctx * CTX_SEQ, n_lat * LAT_SEQ

    n_rows = -(-(1 + n_lat) // 8) * 8
    c_rows = jnp.zeros((n_rows, d), F32).at[0].set(c_ctx).at[1:1 + n_lat].set(c)
    mod = _modulation(c_rows, ada_w, ada_b).reshape(depth * n_rows, 1, 6 * d)

    lb_all = jnp.cumsum(jax.nn.softmax(hgrn_lb_logits.astype(F32), axis=0), axis=0)
    lb_all = (lb_all - lb_all[:1]).reshape(depth, 1, 2 * W_BR)

    head_mask = _block_diag_heads(jnp.ones((H_B, HGRN_HD, HGRN_HD), F32))
    level_masks = jnp.asarray(_level_masks())

    rows = lambda a: a.reshape(depth, 1, -1)
    w = {
        "w_out_a": w_out_a.astype(BF16), "w_out_b": w_out_b.astype(BF16), "w_out_c": w_out_c.astype(BF16),
        "w_pool": _block_diag_heads(pool_w).astype(BF16), "pool_scale": rows(pool_scale),
        "w_o": w_o.astype(BF16), "norm_ffn_g": rows(norm_ffn_g),
        "conv_a_w": conv_a_w, "conv_a_b": rows(conv_a_b), "ln_a_g": rows(ln_a_g), "ln_a_b": rows(ln_a_b),
        "conv_c_w": conv_c_w,
    }
    norm_mix, hgrn_g = rows(norm_mix_g), rows(hgrn_norm_g)
    w13 = ffn_w13.astype(BF16)
    w2 = ffn_w2.astype(BF16)
    final_g = final_norm_g.reshape(1, d)

    y_p = x_prompt.reshape(t_ctx, d)
    y_s = x_sample.reshape(t_lat, d)
    ctx_states = []
    w_mix = w_gate = w_in.astype(BF16)
    for l in range(depth):
        final = l == depth - 1
        outs = []
        for latent, x, tokens_per_mod in ((False, y_p, t_ctx), (True, y_s, LAT_SEQ)):
            mod_row = l * n_rows + (1 if latent else 0)
            h, z = _inproj(x, mod, mod_row, norm_mix, w_mix, lb_all, l, tokens_per_mod)
            hg = _hgrn(z, hgrn_g, head_mask, level_masks, state_hgrn if latent else None, l)
            if not latent:
                ctx_states.append(hg[1])
            x1, h2 = _mix_merge(h, z, hg[0], x, mod, mod_row, w_gate, w, l, tokens_per_mod, latent)
            outs.append(_ffn(h2, x1, mod, mod_row, w13, w2, final_g, l, tokens_per_mod, final))
        y_p, y_s = outs

    new_state = jnp.stack(ctx_states, axis=1).astype(x_prompt.dtype)
    return (y_p.reshape(x_prompt.shape), y_s.reshape(x_sample.shape), new_state)
```

```python
import functools

import numpy as np
import jax
import jax.numpy as jnp
from jax import lax
from jax.experimental import pallas as pl
from jax.experimental.pallas import tpu as pltpu

F32 = jnp.float32
BF16 = jnp.bfloat16

D_MODEL = 1024
GRID_W = 64
EPS = 1e-6
W_BR = 256
CONV_A_K = 31
H_B = 4
HGRN_HD = 64
POOL_WINDOWS = (2, 4, 8, 16)
POOL_GC = 64
D_FF = 2816
N_MIX = 11 * W_BR
N_BRANCH = 4
N_Z = 9 * W_BR
CTX_SEQ = 256
LAT_SEQ = 2048

INPROJ_TM = 1024
MERGE_TM = 512
FFN_TM = 1024
FFN_CHUNK = 256
INPROJ_SUB = 128
MERGE_SUB = 256
FFN_SUB = 512

VMEM_LIMIT = 56 * 1024 * 1024
LOG2E = 1.4426950408889634


def _cparams(n_axes=1):
    return pltpu.CompilerParams(dimension_semantics=("arbitrary",) * n_axes, vmem_limit_bytes=VMEM_LIMIT)


def _sigmoid(x):
    return 1.0 / (1.0 + jnp.exp(-x))


def _silu(x):
    return x * _sigmoid(x)


def _rmsnorm(x, g):
    return x * lax.rsqrt(jnp.mean(x * x, axis=-1, keepdims=True) + EPS) * g


def _dot(a, b):
    return jnp.dot(a, b, preferred_element_type=F32)


def _layer_spec(a, layer):
    return pl.BlockSpec((1,) + a.shape[1:], lambda i: (layer,) + (0,) * (a.ndim - 1))


def _mod_spec(tm, tokens_per_mod, first_row):
    return pl.BlockSpec((1, 1, 6 * D_MODEL), lambda i: (first_row + (i * tm) // tokens_per_mod, 0, 0))


def _mod_kernel(c_ref, w_ref, b_ref, o_ref):
    s = _silu(c_ref[...]).astype(BF16)
    o_ref[0] = _dot(s, w_ref[0].astype(BF16)) + b_ref[0]


def _modulation(c_rows, ada_w, ada_b):
    depth, d, n = ada_w.shape
    nr = c_rows.shape[0]
    tn = 1536
    return pl.pallas_call(
        _mod_kernel,
        grid=(depth, n // tn),
        in_specs=[
            pl.BlockSpec((nr, d), lambda l, j: (0, 0)),
            pl.BlockSpec((1, d, tn), lambda l, j: (l, 0, j)),
            pl.BlockSpec((1, 1, tn), lambda l, j: (l, 0, j)),
        ],
        out_specs=pl.BlockSpec((1, nr, tn), lambda l, j: (l, 0, j)),
        out_shape=jax.ShapeDtypeStruct((depth, nr, n), F32),
        compiler_params=_cparams(2),
        name="modulation",
    )(c_rows, ada_w, ada_b.reshape(depth, 1, n))


def _log2_forget(z, lb):
    zs = z * LOG2E
    ls = jnp.minimum(zs, 0.0) - jnp.log2(1.0 + jnp.exp2(-jnp.abs(zs)))
    a = jnp.log2(lb)
    b = jnp.log1p(-lb) * LOG2E + ls
    return jnp.maximum(a, b) + jnp.log2(1.0 + jnp.exp2(-jnp.abs(a - b)))


def _inproj_kernel(x_ref, mod_ref, g_ref, w_ref, lb_ref, h_ref, z_ref):
    d, w = D_MODEL, W_BR
    mod = mod_ref[0]
    sh, sc = mod[:, 0:d], mod[:, d:2 * d]
    lb = lb_ref[0]
    for r0 in range(0, x_ref.shape[0], INPROJ_SUB):
        rows = slice(r0, r0 + INPROJ_SUB)
        h = (_rmsnorm(x_ref[rows, :], g_ref[0]) * (1.0 + sc) + sh).astype(BF16)
        h_ref[rows, :] = h

        def proj(j, n, h=h):
            return _dot(h, w_ref[:, j * w:(j + n) * w])

        za = proj(0, 2)
        z_ref[rows, 0 * w:1 * w] = za[:, 0:w] * _sigmoid(za[:, w:2 * w])
        zb = proj(2, 3)
        z_ref[rows, 1 * w:2 * w] = zb[:, 0:w]
        z_ref[rows, 2 * w:4 * w] = _log2_forget(zb[:, w:3 * w], lb)
        zv = proj(5, 2)
        z_ref[rows, 4 * w:5 * w] = zv[:, 0:w]
        z_ref[rows, 5 * w:6 * w] = _silu(zv[:, w:2 * w])
        zc = proj(7, 3)
        z_ref[rows, 6 * w:7 * w] = zc[:, 0:w]
        z_ref[rows, 7 * w:8 * w] = zc[:, w:2 * w] * zc[:, 2 * w:3 * w]
        z_ref[rows, 8 * w:9 * w] = proj(10, 1)


def _inproj(x, mod, mod_row, g, w_mix, lb, layer, tokens_per_mod):
    t, d = x.shape
    tm = INPROJ_TM
    return pl.pallas_call(
        _inproj_kernel,
        grid=(t // tm,),
        in_specs=[
            pl.BlockSpec((tm, d), lambda i: (i, 0)),
            _mod_spec(tm, tokens_per_mod, mod_row),
            _layer_spec(g, layer),
            pl.BlockSpec((pl.Squeezed(), d, N_MIX), lambda i: (layer, 0, 0)),
            _layer_spec(lb, layer),
        ],
        out_specs=[
            pl.BlockSpec((tm, d), lambda i: (i, 0)),
            pl.BlockSpec((tm, N_Z), lambda i: (i, 0)),
        ],
        out_shape=[jax.ShapeDtypeStruct((t, d), BF16), jax.ShapeDtypeStruct((t, N_Z), F32)],
        compiler_params=_cparams(),
        name="inproj",
    )(x, mod, g, w_mix, lb)


def _row_index(rows):
    return lax.broadcasted_iota(jnp.int32, (rows, W_BR), 0)


def _shift_rows(x, j, pos, seg):
    if j == 0:
        return x
    y = pltpu.roll(x, (-j) % x.shape[0], axis=0)
    valid = (pos < seg - j) if j > 0 else (pos >= -j)
    return jnp.where(valid, y, 0.0)


def _conv_a_ln(x, wa_ref, ba_ref, lg_ref, lbeta_ref):
    half = CONV_A_K // 2
    seg = x.shape[0]
    pad = 2 * SUBLANES
    zeros = jnp.zeros((pad, x.shape[1]), F32)
    xpad = jnp.concatenate([zeros, x, zeros], axis=0)
    n = seg + 2 * pad
    rotated = [xpad] + [pltpu.roll(xpad, n - r, axis=0) for r in range(1, SUBLANES)]
    acc = None
    for j in range(-half, half + 1):
        whole, r = divmod(pad + j, SUBLANES)
        term = rotated[r][whole * SUBLANES:whole * SUBLANES + seg] * wa_ref[half + j:half + j + 1, :]
        acc = term if acc is None else acc + term
    u = acc + ba_ref[...]
    mu = jnp.mean(u, axis=-1, keepdims=True)
    var = jnp.mean(jnp.square(u - mu), axis=-1, keepdims=True)
    return _silu((u - mu) * lax.rsqrt(var + EPS) * lg_ref[...] + lbeta_ref[...])


LANE_TILE = 128
SUBLANES = 8


def _pool_lane_tiles():
    per_tile = LANE_TILE // POOL_GC
    tiles = []
    for i in range(W_BR // LANE_TILE):
        lane = lax.broadcasted_iota(jnp.int32, (1, LANE_TILE), 1) + i * LANE_TILE
        half = jnp.left_shift(1, lane // POOL_GC)
        reach = max(POOL_WINDOWS[i * per_tile:(i + 1) * per_tile]) // 2
        tiles.append((slice(i * LANE_TILE, (i + 1) * LANE_TILE), half, reach))
    return tiles


SB = 256
SUB = 8
LEVELS = (8, 16, 32, 64, 128)
NQ = SB // 2
CTX_PER_STEP = 2


def _level_spans(m, upper):
    off = m if upper else 0
    return [(i * 2 * m + off, i * 2 * m + off + m) for i in range(SB // (2 * m))]


def _take(x, spans):
    parts = [x[a:b] for a, b in spans]
    return parts[0] if len(parts) == 1 else jnp.concatenate(parts, axis=0)


def _level_masks():
    tq = np.arange(H_B * NQ)[:, None] % NQ
    ts = np.arange(NQ)[None, :]
    return np.stack([((tq // m) == (ts // m)) for m in LEVELS[:-1]]).astype(np.float32)


def _cumsum_rows(x):
    g = SB // SUB
    x3 = x.reshape(g, SUB, x.shape[-1])
    sub = lax.broadcasted_iota(jnp.int32, x3.shape, 1)
    step = 1
    while step < SUB:
        x3 = x3 + jnp.where(sub >= step, pltpu.roll(x3, step, axis=1), 0.0)
        step *= 2
    tot = x3[:, SUB - 1:SUB, :]
    offs = [jnp.zeros_like(tot[0:1])]
    for i in range(1, g):
        offs.append(offs[-1] + tot[i - 1:i])
    return (x3 + jnp.concatenate(offs, axis=0)).reshape(x.shape)


def _hgrn_kernel(*refs, has_s0, has_prev):
    n_prev = 0
    if has_s0:
        q_ref, lff_ref, lfb_ref, v_ref, sg_ref, ng_ref, hm_ref, lm_ref, s0_ref, o_ref, oacc, st = refs
        sfin_ref = None
    else:
        q_ref, lff_ref, lfb_ref, v_ref, sg_ref, ng_ref, hm_ref, lm_ref, *prev, o_ref, sfin_ref, oacc, st = refs
        s0_ref = None
        if has_prev:
            n_prev = prev[0].shape[1]
            sfin_ref[:, 0:n_prev] = prev[0][...]
    seq, w = q_ref.shape
    n_sb = seq // SB
    with_inter = has_s0
    head_ones = hm_ref[...].astype(BF16)
    lane_head = lax.broadcasted_iota(jnp.int32, (1, w), 1) // HGRN_HD
    in_head = [lane_head == h for h in range(H_B)]
    sub3 = lax.broadcasted_iota(jnp.int32, (SB // SUB, SUB, w), 1)
    nt = (((1,), (1,)), ((), ()))
    tn = (((0,), (0,)), ((), ()))
    head_blocks = [slice(h * HGRN_HD, (h + 1) * HGRN_HD) for h in range(H_B)]

    for direction in range(2 if with_inter else 0):
        st[direction] = jnp.zeros((w, w), F32)
        if s0_ref is not None:
            for h, blk in enumerate(head_blocks):
                st[direction, blk, blk] = s0_ref[0, 0, direction, h]
            st[direction] = st[direction].T

    def sb_body(si, carry):
        for direction, lf_ref in ((0, lff_ref), (1, lfb_ref)):
            back = direction == 1
            base = pl.multiple_of(((n_sb - 1 - si) if back else si) * SB, SB)
            rows = pl.ds(base, SB)
            lf = lf_ref[rows, :]
            q = q_ref[rows, :]
            v = v_ref[rows, :]
            vb = v.astype(BF16)
            key = 1.0 - jnp.exp2(lf)
            g = _cumsum_rows(lf)
            tot = g[SB - 1:SB, :]
            b = (lf - g) if back else g

            q3, k3, b3, v3 = (x.reshape(SB // SUB, SUB, w) for x in (q, key, b, v))
            acc = _dot((q * key).astype(BF16), head_ones) * v
            for dist in range(1, SUB):
                sft = (SUB - dist) if back else dist
                valid = (sub3 < SUB - dist) if back else (sub3 >= dist)
                e = jnp.where(valid, q3 * pltpu.roll(k3, sft, axis=1) * jnp.exp2(b3 - pltpu.roll(b3, sft, axis=1)), 0.0)
                red = _dot(e.reshape(SB, w).astype(BF16), head_ones)
                acc = acc + red * pltpu.roll(v3, sft, axis=1).reshape(SB, w)
            oacc[direction, rows, :] = acc

            for li, m in enumerate(LEVELS):
                lower, upper = _level_spans(m, False), _level_spans(m, True)
                qs, ks = (lower, upper) if back else (upper, lower)
                edge = [(a + m) if back else (a + m - 1) for a, _ in lower]
                bm = jnp.concatenate([jnp.broadcast_to(b[r:r + 1], (m, w)) for r in edge], axis=0)
                qg = _take(q, qs) * jnp.exp2(_take(b, qs) - bm)
                kg = (_take(key, ks) * jnp.exp2(bm - _take(b, ks))).astype(BF16)
                stacked = jnp.concatenate([jnp.where(in_head[h], qg, 0.0) for h in range(H_B)], axis=0)
                p = lax.dot_general(stacked.astype(BF16), kg, nt, preferred_element_type=F32)
                if m < NQ:
                    p = p * lm_ref[li]
                r = _dot(p.astype(BF16), _take(vb, ks))
                res = jnp.where(in_head[0], r[0:NQ], 0.0)
                for h in range(1, H_B):
                    res = res + jnp.where(in_head[h], r[h * NQ:(h + 1) * NQ], 0.0)
                for bi, (a0, _) in enumerate(qs):
                    dst = pl.ds(base + a0, m)
                    oacc[direction, dst, :] = oacc[direction, dst, :] + res[bi * m:(bi + 1) * m]

            cq, ck = (tot, 0.0) if back else (0.0, tot)
            kp = (key * jnp.exp2(ck - b)).astype(BF16)
            if with_inter:
                s_old = st[direction]
                qp = (q * jnp.exp2(b + cq)).astype(BF16)
                oacc[direction, rows, :] = oacc[direction, rows, :] + lax.dot_general(
                    qp, s_old.astype(BF16), nt, preferred_element_type=F32)
                kv_t = lax.dot_general(vb, kp, tn, preferred_element_type=F32)
                st[direction] = s_old * jnp.exp2(tot) + kv_t * hm_ref[...]
            else:
                kv = lax.dot_general(kp, vb, tn, preferred_element_type=F32)
                for h, blk in enumerate(head_blocks):
                    sfin_ref[(n_sb - 1 - si) if back else si, n_prev, direction, h] = kv[blk, blk]
        return carry

    lax.fori_loop(0, n_sb, sb_body, 0, unroll=2 if n_sb % 2 == 0 else 1)

    def out_body(i, carry):
        rows = pl.ds(pl.multiple_of(i * SB, SB), SB)
        o = oacc[0, rows, :] + oacc[1, rows, :]
        o2 = o * o
        hi = o2.astype(BF16)
        lo = (o2 - hi.astype(F32)).astype(BF16)
        ms = (_dot(hi, head_ones) + _dot(lo, head_ones)) * (1.0 / HGRN_HD)
        o_ref[rows, :] = (o * lax.rsqrt(ms + EPS) * ng_ref[0] * sg_ref[rows, :]).astype(BF16)
        return carry

    lax.fori_loop(0, n_sb, out_body, 0)


def _hgrn(z, norm_g, head_mask, level_masks, s0, layer, prev_states=None):
    t = z.shape[0]
    if s0 is None:
        assert CTX_SEQ == SB
        n_seq, seq = t // CTX_SEQ, CTX_SEQ * CTX_PER_STEP
        n_prev = 0 if prev_states is None else prev_states.shape[1]
        state_block = lambda layers: (CTX_PER_STEP, layers, 2, H_B, HGRN_HD, HGRN_HD)
    else:
        n_seq, seq = t // LAT_SEQ, LAT_SEQ
    zspec = lambda col: pl.BlockSpec((seq, W_BR), lambda i, col=col: (i, col))
    wb = H_B * HGRN_HD
    in_specs = [zspec(1), zspec(2), zspec(3), zspec(4), zspec(5),
                _layer_spec(norm_g, layer),
                pl.BlockSpec((wb, wb), lambda i: (0, 0)),
                pl.BlockSpec(level_masks.shape, lambda i: (0, 0, 0))]
    args = [z, z, z, z, z, norm_g, head_mask, level_masks]
    out_specs = [pl.BlockSpec((seq, wb), lambda i: (i, 0))]
    out_shape = [jax.ShapeDtypeStruct((t, wb), BF16)]
    if s0 is not None:
        in_specs.append(pl.BlockSpec((1, 1, 2, H_B, HGRN_HD, HGRN_HD), lambda i: (i, layer, 0, 0, 0, 0)))
        args.append(s0)
    else:
        if n_prev:
            in_specs.append(pl.BlockSpec(state_block(n_prev), lambda i: (i, 0, 0, 0, 0, 0)))
            args.append(prev_states)
        out_specs.append(pl.BlockSpec(state_block(n_prev + 1), lambda i: (i, 0, 0, 0, 0, 0)))
        out_shape.append(jax.ShapeDtypeStruct((n_seq, n_prev + 1, 2, H_B, HGRN_HD, HGRN_HD), F32))
    return pl.pallas_call(
        functools.partial(_hgrn_kernel, has_s0=s0 is not None, has_prev=s0 is None and n_prev > 0),
        grid=(t // seq,),
        in_specs=in_specs,
        out_specs=out_specs,
        out_shape=out_shape,
        scratch_shapes=[pltpu.VMEM((2, seq, wb), F32), pltpu.VMEM((2, wb, wb), F32)],
        compiler_params=_cparams(),
        name="hgrn_lat" if s0 is not None else "hgrn_ctx",
    )(*args)


TILE_ROWS = MERGE_TM // GRID_W
SUB_ROWS = MERGE_SUB // GRID_W


def _mix_ctx_segment(ua_ref, cb_ref, cc_ref, d_ref, rows, prm, pos, posl, tiles):
    wa_ref, ba_ref, lg_ref, lbeta_ref, wc_ref = prm
    seg = CTX_SEQ
    u_a = _conv_a_ln(ua_ref[rows, :], wa_ref, ba_ref, lg_ref, lbeta_ref).astype(BF16)
    cc = cc_ref[rows, :]
    u = (_shift_rows(cc, -1, pos, seg) * wc_ref[0:1, :] + cc * wc_ref[1:2, :]
         + _shift_rows(cc, 1, pos, seg) * wc_ref[2:3, :])
    u_c = (cb_ref[rows, :] * u).astype(BF16)
    parts = []
    for lanes, half, reach in tiles:
        dd = d_ref[rows, lanes]
        acc = jnp.zeros_like(dd)
        cnt = jnp.zeros_like(dd)
        for j in range(-reach, reach):
            lane_on = (j >= -half) & (j <= half - 1)
            valid = (posl + j >= 0) & (posl + j < seg) & lane_on
            acc = acc + jnp.where(valid, _shift_rows(dd, j, posl, seg), 0.0)
            cnt = cnt + jnp.where(valid, 1.0, 0.0)
        parts.append((acc / cnt - dd).astype(BF16))
    return u_a, u_c, jnp.concatenate(parts, axis=1)


def _mix_lat_rows(ua_ref, cb_ref, cc3, d3, first, seq_row0, prm, tiles):
    wa_ref, ba_ref, lg_ref, lbeta_ref, wc_ref = prm
    n_rows = LAT_SEQ // GRID_W

    def grid_row(refs, rr, lanes=slice(None)):
        ref, idx = (refs[0], rr + TILE_ROWS) if rr < 0 else (refs[2], rr - TILE_ROWS) if rr >= TILE_ROWS \
            else (refs[1], rr)
        ok = jnp.where(jnp.logical_and(seq_row0 + rr >= 0, seq_row0 + rr < n_rows), 1.0, 0.0)
        return ref[idx * GRID_W:(idx + 1) * GRID_W, lanes], ok

    out_a, out_c, out_d = [], [], []
    for r in range(first, first + SUB_ROWS):
        rows = slice(r * GRID_W, (r + 1) * GRID_W)
        out_a.append(_conv_a_ln(ua_ref[rows, :], wa_ref, ba_ref, lg_ref, lbeta_ref).astype(BF16))
        up, ok_up = grid_row(cc3, r - 1)
        dn, ok_dn = grid_row(cc3, r + 1)
        u = up * (wc_ref[0:1, :] * ok_up) + cc3[1][rows, :] * wc_ref[1:2, :] + dn * (wc_ref[2:3, :] * ok_dn)
        out_c.append((cb_ref[rows, :] * u).astype(BF16))
        parts = []
        for lanes, half, reach in tiles:
            acc = jnp.zeros((GRID_W, LANE_TILE), F32)
            cnt = jnp.zeros((1, LANE_TILE), F32)
            for j in range(-reach, reach):
                lane_on = jnp.where((j >= -half) & (j <= half - 1), 1.0, 0.0)
                x, ok = grid_row(d3, r + j, lanes)
                wgt = lane_on * ok
                acc = acc + x * wgt
                cnt = cnt + wgt
            parts.append((acc / cnt - d3[1][rows, lanes]).astype(BF16))
        out_d.append(jnp.concatenate(parts, axis=1))
    cat = lambda xs: jnp.concatenate(xs, axis=0)
    return cat(out_a), cat(out_c), cat(out_d)


def _mix_merge_kernel(*refs, latent):
    if latent:
        (h_ref, ua_ref, cb_ref, ccp_ref, cc_ref, ccn_ref, dp_ref, d_ref, dn_ref, ub_ref, x_ref, mod_ref, wg_ref,
         *prm, wa_ref, wb_ref, wc_ref, wd_ref, ps_ref, wo_ref, g2_ref, x1_ref, h2_ref) = refs
    else:
        (h_ref, ua_ref, cb_ref, cc_ref, d_ref, ub_ref, x_ref, mod_ref, wg_ref,
         *prm, wa_ref, wb_ref, wc_ref, wd_ref, ps_ref, wo_ref, g2_ref, x1_ref, h2_ref) = refs
    prm = [r.at[0] for r in prm]
    d = D_MODEL
    mod = mod_ref[0]
    g_m, sh_f, sc_f = mod[:, 2 * d:3 * d], mod[:, 3 * d:4 * d], mod[:, 4 * d:5 * d]
    tiles = _pool_lane_tiles()
    if latent:
        seq_row0 = (pl.program_id(0) % (LAT_SEQ // MERGE_TM)) * TILE_ROWS
    else:
        pos = _row_index(CTX_SEQ)
        posl = lax.broadcasted_iota(jnp.int32, (CTX_SEQ, LANE_TILE), 0)
    for s, r0 in enumerate(range(0, x_ref.shape[0], MERGE_SUB)):
        rows = slice(r0, r0 + MERGE_SUB)
        if latent:
            u_a, u_c, u_d = _mix_lat_rows(ua_ref, cb_ref, (ccp_ref, cc_ref, ccn_ref), (dp_ref, d_ref, dn_ref),
                                          s * SUB_ROWS, seq_row0, prm, tiles)
        else:
            u_a, u_c, u_d = _mix_ctx_segment(ua_ref, cb_ref, cc_ref, d_ref, rows, prm, pos, posl, tiles)
        h = h_ref[rows, :]
        merged = None
        for i, u, w_ref in ((1, ub_ref[rows, :], wb_ref), (0, u_a, wa_ref), (2, u_c, wc_ref), (3, u_d, wd_ref)):
            y = _dot(u, w_ref[0])
            if i == 3:
                y = y * ps_ref[0]
            term = _sigmoid(_dot(h, wg_ref[0, :, i * d:(i + 1) * d])) * y
            merged = term if merged is None else merged + term
        mix = _dot(merged.astype(BF16), wo_ref[0])
        x1 = x_ref[rows, :] + g_m * mix
        x1_ref[rows, :] = x1
        h2_ref[rows, :] = (_rmsnorm(x1, g2_ref[0]) * (1.0 + sc_f) + sh_f).astype(BF16)


def _mix_merge(h, z, u_b, x, mod, mod_row, w_gate, w, layer, tokens_per_mod, latent):
    t, d = x.shape
    tm = MERGE_TM
    n_tiles = t // tm
    assert MERGE_SUB == CTX_SEQ and TILE_ROWS >= max(POOL_WINDOWS) // 2 and LAT_SEQ % tm == 0
    tok = lambda width: pl.BlockSpec((tm, width), lambda i: (i, 0))
    zcol = lambda col: pl.BlockSpec((tm, W_BR), lambda i, col=col: (i, col))
    above = lambda col: pl.BlockSpec((tm, W_BR), lambda i, col=col: (jnp.maximum(i - 1, 0), col))
    below = lambda col: pl.BlockSpec((tm, W_BR), lambda i, col=col: (jnp.minimum(i + 1, n_tiles - 1), col))
    if latent:
        z_specs = [zcol(0), zcol(6), above(7), zcol(7), below(7), above(8), zcol(8), below(8)]
    else:
        z_specs = [zcol(0), zcol(6), zcol(7), zcol(8)]
    conv = [w["conv_a_w"], w["conv_a_b"], w["ln_a_g"], w["ln_a_b"], w["conv_c_w"]]
    weights = [w["w_out_a"], w["w_out_b"], w["w_out_c"], w["w_pool"], w["pool_scale"], w["w_o"], w["norm_ffn_g"]]
    return pl.pallas_call(
        functools.partial(_mix_merge_kernel, latent=latent),
        grid=(n_tiles,),
        in_specs=[tok(d)] + z_specs + [tok(W_BR), tok(d), _mod_spec(tm, tokens_per_mod, mod_row),
                                       pl.BlockSpec((pl.Element(1), pl.Element(d), pl.Element(N_BRANCH * d)),
                                                    lambda i: (layer, 0, N_MIX))]
                 + [_layer_spec(a, layer) for a in conv + weights],
        out_specs=[tok(d), tok(d)],
        out_shape=[jax.ShapeDtypeStruct((t, d), F32), jax.ShapeDtypeStruct((t, d), BF16)],
        compiler_params=_cparams(),
        name="merge_lat" if latent else "merge_ctx",
    )(h, *([z] * len(z_specs)), u_b, x, mod, w_gate, *conv, *weights)


def _ffn_kernel(h2_ref, x1_ref, mod_ref, w13_ref, w2_ref, gf_ref, o_ref, *, final):
    d = D_MODEL
    w13_ref, w2_ref = w13_ref.at[0], w2_ref.at[0]
    g_f = mod_ref[0][:, 5 * d:6 * d]
    for r0 in range(0, x1_ref.shape[0], FFN_SUB):
        rows = slice(r0, r0 + FFN_SUB)
        h2 = h2_ref[rows, :]
        acc = None
        for lo in range(0, D_FF, FFN_CHUNK):
            hi = min(lo + FFN_CHUNK, D_FF)
            gate = _dot(h2, w13_ref[:, lo:hi])
            up = _dot(h2, w13_ref[:, D_FF + lo:D_FF + hi])
            part = _dot((_silu(gate) * up).astype(BF16), w2_ref[lo:hi, :])
            acc = part if acc is None else acc + part
        x2 = x1_ref[rows, :] + g_f * acc
        if final:
            x2 = _rmsnorm(x2, gf_ref[...])
        o_ref[rows, :] = x2


def _ffn(h2, x1, mod, mod_row, w13, w2, final_g, layer, tokens_per_mod, final):
    t, d = x1.shape
    tm = FFN_TM
    return pl.pallas_call(
        functools.partial(_ffn_kernel, final=final),
        grid=(t // tm,),
        in_specs=[pl.BlockSpec((tm, d), lambda i: (i, 0)), pl.BlockSpec((tm, d), lambda i: (i, 0)),
                  _mod_spec(tm, tokens_per_mod, mod_row), _layer_spec(w13, layer), _layer_spec(w2, layer),
                  pl.BlockSpec(final_g.shape, lambda i: (0, 0))],
        out_specs=pl.BlockSpec((tm, d), lambda i: (i, 0)),
        out_shape=jax.ShapeDtypeStruct((t, d), F32),
        compiler_params=_cparams(),
        name="ffn",
    )(h2, x1, mod, w13, w2, final_g)


def _block_diag_heads(s):
    eye = jnp.eye(H_B, dtype=s.dtype)
    bd = jnp.einsum("...hab,hg->...hagb", s, eye)
    return bd.reshape(s.shape[:-3] + (H_B * s.shape[-2], H_B * s.shape[-1]))


def kernel(x_prompt, x_sample, state_hgrn, c, c_ctx, ada_w, ada_b, norm_mix_g, w_in, conv_a_w, conv_a_b, ln_a_g,
           ln_a_b, w_out_a, hgrn_lb_logits, hgrn_norm_g, w_out_b, conv_c_w, w_out_c, pool_w, pool_scale, w_o,
           norm_ffn_g, ffn_w13, ffn_w2, final_norm_g):
    depth = w_in.shape[0]
    d = D_MODEL
    n_ctx, n_lat = x_prompt.shape[0], x_sample.shape[0]
    t_ctx, t_lat = n_ctx * CTX_SEQ, n_lat * LAT_SEQ

    n_rows = -(-(1 + n_lat) // 8) * 8
    c_rows = jnp.zeros((n_rows, d), F32).at[0].set(c_ctx).at[1:1 + n_lat].set(c)
    mod = _modulation(c_rows, ada_w, ada_b).reshape(depth * n_rows, 1, 6 * d)

    lb_all = jnp.cumsum(jax.nn.softmax(hgrn_lb_logits.astype(F32), axis=0), axis=0)
    lb_all = (lb_all - lb_all[:1]).reshape(depth, 1, 2 * W_BR)

    head_mask = _block_diag_heads(jnp.ones((H_B, HGRN_HD, HGRN_HD), F32))
    level_masks = jnp.asarray(_level_masks())

    rows = lambda a: a.reshape(depth, 1, -1)
    w = {
        "w_out_a": w_out_a.astype(BF16), "w_out_b": w_out_b.astype(BF16), "w_out_c": w_out_c.astype(BF16),
        "w_pool": _block_diag_heads(pool_w).astype(BF16), "pool_scale": rows(pool_scale),
        "w_o": w_o.astype(BF16), "norm_ffn_g": rows(norm_ffn_g),
        "conv_a_w": conv_a_w, "conv_a_b": rows(conv_a_b), "ln_a_g": rows(ln_a_g), "ln_a_b": rows(ln_a_b),
        "conv_c_w": conv_c_w,
    }
    norm_mix, hgrn_g = rows(norm_mix_g), rows(hgrn_norm_g)
    w13 = ffn_w13.astype(BF16)
    w2 = ffn_w2.astype(BF16)
    final_g = final_norm_g.reshape(1, d)

    y_p = x_prompt.reshape(t_ctx, d)
    y_s = x_sample.reshape(t_lat, d)
    ctx_states = None
    w_mix = w_gate = w_in.astype(BF16)
    for l in range(depth):
        final = l == depth - 1
        outs = []
        for latent, x, tokens_per_mod in ((False, y_p, t_ctx), (True, y_s, LAT_SEQ)):
            mod_row = l * n_rows + (1 if latent else 0)
            h, z = _inproj(x, mod, mod_row, norm_mix, w_mix, lb_all, l, tokens_per_mod)
            hg = _hgrn(z, hgrn_g, head_mask, level_masks, state_hgrn if latent else None, l,
                       None if latent else ctx_states)
            if not latent:
                ctx_states = hg[1]
            x1, h2 = _mix_merge(h, z, hg[0], x, mod, mod_row, w_gate, w, l, tokens_per_mod, latent)
            outs.append(_ffn(h2, x1, mod, mod_row, w13, w2, final_g, l, tokens_per_mod, final))
        y_p, y_s = outs

    new_state = ctx_states.astype(x_prompt.dtype)
    return (y_p.reshape(x_prompt.shape), y_s.reshape(x_sample.shape), new_state)
```
